```python
import math
import jax, jax.numpy as jnp
from jax import lax
import numpy as np

D_MODEL = 1024
BATCH = 32
SEQ = 2048
DEPTH = 1

HEAD_DIM = 64
MOBA_HEADS = 8
MOBA_WIDTH = MOBA_HEADS * HEAD_DIM
MOBA_BLOCK = 256
MOBA_TOPK = 3
MOBA_QCHUNK = 128
DIFF_HEADS = 4
DIFF_QK_DIM = HEAD_DIM
DIFF_V_DIM = 2 * HEAD_DIM
DIFF_WIDTH = DIFF_HEADS * DIFF_V_DIM
MIX_WIDTH = MOBA_WIDTH + DIFF_WIDTH
DIFF_QK_COLS = DIFF_HEADS * 2 * DIFF_QK_DIM
IN_COLS = 3 * MOBA_WIDTH + 2 * DIFF_QK_COLS + DIFF_WIDTH
ATTN_QBLOCK = 128
MEM_LEN = 256
MEM_HEADS = 4
MEM_HEAD_DIM = D_MODEL // MEM_HEADS
N_GROUPS = 4
EXPERTS_PER_GROUP = 8
N_EXPERTS = N_GROUPS * EXPERTS_PER_GROUP
TOPK_IN_GROUP = 2
EXPERT_FF = D_MODEL // 2
RMS_EPS = 1e-6
NEG_INF = -1e30

kernel_name = 'hymba_moba_diff_hmoe_block'


def _rmsnorm(x, g):
    xf = x.astype(jnp.float32)
    y = xf * lax.rsqrt(jnp.mean(xf * xf, axis=-1, keepdims=True) + RMS_EPS)
    return (y * g.astype(jnp.float32)).astype(x.dtype)


def _alibi_slopes(n):
    return jnp.asarray(np.array([2.0 ** (-8.0 * (i + 1) / n) for i in range(n)], dtype=np.float32))


def _lambda_init(layer):
    return 0.8 - 0.6 * math.exp(-0.3 * layer)


def _moba_attention(q, k, v):
    B, S, H, D = q.shape
    L = MOBA_BLOCK
    QC = MOBA_QCHUNK
    nblk = -(-S // L)
    s_pad = nblk * L
    nchunk = S // QC
    n_sel = min(MOBA_TOPK, nblk)
    scale = D ** -0.5
    slopes = _alibi_slopes(H)
    pad = ((0, 0), (0, s_pad - S), (0, 0), (0, 0))
    kb = jnp.pad(k, pad).reshape(B, nblk, L, H, D).transpose(0, 3, 1, 2, 4)
    vb = jnp.pad(v, pad).reshape(B, nblk, L, H, D).transpose(0, 3, 1, 2, 4)
    k_mean = jnp.mean(kb.astype(jnp.float32), axis=3)
    gate = jnp.einsum('bshd,bhnd->bhsn', q.astype(jnp.float32), k_mean)
    q_blk = jnp.arange(S) // L
    fully_past = jnp.arange(nblk)[None, :] < q_blk[:, None]
    gate = jnp.where(fully_past[None, None], gate, NEG_INF)
    _, sel = lax.top_k(gate, n_sel)
    q_items = q.reshape(B, nchunk, QC, H, D).transpose(0, 1, 3, 2, 4).reshape(B * nchunk, H, QC, D)
    sel_items = sel.reshape(B, H, nchunk, QC, n_sel).transpose(0, 2, 1, 3, 4).reshape(B * nchunk, H, QC, n_sel)
    b_items = jnp.repeat(jnp.arange(B, dtype=jnp.int32), nchunk)
    c_items = jnp.tile(jnp.arange(nchunk, dtype=jnp.int32), B)
    offs = jnp.arange(L)
    h_idx = jnp.arange(H)[:, None, None]
    slot = jnp.arange(n_sel)

    def one_chunk(item):
        qc, sc, bi, ci = item
        kbb = kb[bi]
        vbb = vb[bi]
        t = ci * QC + jnp.arange(QC)
        own = (ci * QC) // L
        k_g = kbb[h_idx, sc]
        v_g = vbb[h_idx, sc]
        k_own = lax.dynamic_index_in_dim(kbb, own, axis=1, keepdims=False)
        v_own = lax.dynamic_index_in_dim(vbb, own, axis=1, keepdims=False)
        s_g = jnp.einsum('hqd,hqnld->hqnl', qc, k_g, preferred_element_type=jnp.float32) * scale
        s_own = jnp.einsum('hqd,hld->hql', qc, k_own, preferred_element_type=jnp.float32) * scale
        dist_g = (t[None, :, None, None] - (sc[..., None] * L + offs)).astype(jnp.float32)
        dist_own = (t[:, None] - (own * L + offs)[None, :]).astype(jnp.float32)
        s_g = s_g - slopes[:, None, None, None] * dist_g
        s_own = s_own - slopes[:, None, None] * dist_own[None]
        s_g = jnp.where((slot < own)[None, None, :, None], s_g, NEG_INF)
        s_own = jnp.where(dist_own[None] >= 0, s_own, NEG_INF)
        scores = jnp.concatenate([s_g.reshape(H, QC, n_sel * L), s_own], axis=-1)
        p = jax.nn.softmax(scores, axis=-1)
        p_g = p[..., :n_sel * L].reshape(H, QC, n_sel, L)
        p_own = p[..., n_sel * L:]
        o = (jnp.einsum('hqnl,hqnld->hqd', p_g, v_g.astype(jnp.float32))
             + jnp.einsum('hql,hld->hqd', p_own, v_own.astype(jnp.float32)))
        return o.astype(qc.dtype)

    o = lax.map(one_chunk, (q_items, sel_items, b_items, c_items))
    return o.reshape(B, nchunk, H, QC, D).transpose(0, 1, 3, 2, 4).reshape(B, S, H * D)


def _diff_attention(q, k, v, lam, subln_w, lam_init):
    B, S, H, _, Dq = q.shape
    QB = ATTN_QBLOCK
    nqb = S // QB
    scale = Dq ** -0.5
    slopes = _alibi_slopes(H)
    k_pos = jnp.arange(S)
    vf = v.astype(jnp.float32)
    q_blocks = q.reshape(B, nqb, QB, H, 2, Dq).transpose(1, 0, 2, 3, 4, 5)

    def one_block(item):
        qb, i = item
        t = i * QB + jnp.arange(QB)
        dist = (t[:, None] - k_pos[None, :]).astype(jnp.float32)
        s = jnp.einsum('bqhcd,bkhcd->bhcqk', qb, k, preferred_element_type=jnp.float32) * scale
        s = s - slopes[None, :, None, None, None] * dist
        s = jnp.where(dist >= 0, s, NEG_INF)
        p = jax.nn.softmax(s, axis=-1)
        a = p[:, :, 0] - lam * p[:, :, 1]
        return jnp.einsum('bhqk,bkhd->bqhd', a, vf)

    o = lax.map(one_block, (q_blocks, jnp.arange(nqb)))
    o = o.transpose(1, 0, 2, 3, 4).reshape(B, S, H, -1)
    o = _rmsnorm(o, subln_w) * (1.0 - lam_init)
    return o.reshape(B, S, -1).astype(v.dtype)


def _memory_cross_attention(xn, memn, w_q, w_kv, w_o):
    B, S, _ = xn.shape
    M = memn.shape[1]
    q = (xn @ w_q).reshape(B, S, MEM_HEADS, MEM_HEAD_DIM)
    kv = (memn @ w_kv).reshape(B, M, 2, MEM_HEADS, MEM_HEAD_DIM)
    s = jnp.einsum('bshd,bmhd->bhsm', q, kv[:, :, 0], preferred_element_type=jnp.float32) * MEM_HEAD_DIM ** -0.5
    p = jax.nn.softmax(s, axis=-1)
    o = jnp.einsum('bhsm,bmhd->bshd', p, kv[:, :, 1].astype(jnp.float32))
    return o.reshape(B, S, D_MODEL).astype(xn.dtype) @ w_o


def _hierarchical_moe(xn, w_rg, b_rg, w_re, b_re, w_gate, w_up, w_down):
    B, S, D = xn.shape
    xf = xn.astype(jnp.float32)
    g_logits = xf @ w_rg.astype(jnp.float32) + b_rg.astype(jnp.float32)
    g_prob = jax.nn.softmax(g_logits, axis=-1)
    g_onehot = jax.nn.one_hot(jnp.argmax(g_logits, axis=-1), N_GROUPS, dtype=jnp.float32)
    g_w = jnp.sum(g_prob * g_onehot, axis=-1, keepdims=True)
    e_logits = (xf @ w_re.astype(jnp.float32) + b_re.astype(jnp.float32)).reshape(B, S, N_GROUPS, EXPERTS_PER_GROUP)
    e_in = jnp.sum(e_logits * g_onehot[..., None], axis=2)
    top_v, top_i = lax.top_k(e_in, TOPK_IN_GROUP)
    top_w = jax.nn.softmax(top_v, axis=-1)
    within = jnp.sum(jax.nn.one_hot(top_i, EXPERTS_PER_GROUP, dtype=jnp.float32) * top_w[..., None], axis=-2)
    combine = (g_onehot[..., :, None] * within[..., None, :] * g_w[..., None]).reshape(B, S, N_EXPERTS)
    w_down_f = w_down.astype(jnp.float32)

    def per_row(item):
        xr, cr = item
        hg = jnp.einsum('sd,edf->sef', xr, w_gate, preferred_element_type=jnp.float32)
        hu = jnp.einsum('sd,edf->sef', xr, w_up, preferred_element_type=jnp.float32)
        hh = jax.nn.silu(hg) * hu * cr[..., None]
        return jnp.einsum('sef,efd->sd', hh, w_down_f).astype(xr.dtype)

    return lax.map(per_row, (xn, combine))


def setup_inputs(seed: int = 0) -> dict:
    key = jax.random.key(seed)
    ks = jax.random.split(key, 25)
    f32 = jnp.float32

    def nrm(k, shape, scale):
        return jax.random.normal(k, shape, f32) * scale

    def gain(k, shape):
        return 1.0 + 0.02 * jax.random.normal(k, shape, f32)

    L = DEPTH
    D = D_MODEL
    return {
        'x': nrm(ks[0], (BATCH, SEQ, D), 1.0),
        'mem': nrm(ks[1], (BATCH, MEM_LEN, D), 1.0),
        'norm_mix': gain(ks[2], (L, D)),
        'w_in': nrm(ks[3], (L, D, IN_COLS), D ** -0.5),
        'lambda_q1': nrm(ks[4], (L, DIFF_QK_DIM), 0.1),
        'lambda_k1': nrm(ks[5], (L, DIFF_QK_DIM), 0.1),
        'lambda_q2': nrm(ks[6], (L, DIFF_QK_DIM), 0.1),
        'lambda_k2': nrm(ks[7], (L, DIFF_QK_DIM), 0.1),
        'diff_subln': gain(ks[8], (L, DIFF_V_DIM)),
        'norm_moba_out': gain(ks[9], (L, MOBA_WIDTH)),
        'w_out': nrm(ks[10], (L, MIX_WIDTH, D), MIX_WIDTH ** -0.5),
        'norm_mem_q': gain(ks[11], (L, D)),
        'norm_mem_kv': gain(ks[12], (L, D)),
        'w_mem_q': nrm(ks[13], (L, D, D), D ** -0.5),
        'w_mem_kv': nrm(ks[14], (L, D, 2 * D), D ** -0.5),
        'w_mem_o': nrm(ks[15], (L, D, D), D ** -0.5),
        'norm_ffn': gain(ks[16], (L, D)),
        'w_router_group': nrm(ks[17], (L, D, N_GROUPS), D ** -0.5),
        'b_router_group': nrm(ks[18], (L, N_GROUPS), 0.01),
        'w_router_expert': nrm(ks[19], (L, D, N_EXPERTS), D ** -0.5),
        'b_router_expert': nrm(ks[20], (L, N_EXPERTS), 0.01),
        'w_expert_gate': nrm(ks[21], (L, N_EXPERTS, D, EXPERT_FF), D ** -0.5),
        'w_expert_up': nrm(ks[22], (L, N_EXPERTS, D, EXPERT_FF), D ** -0.5),
        'w_expert_down': nrm(ks[23], (L, N_EXPERTS, EXPERT_FF, D), EXPERT_FF ** -0.5),
        'norm_final': gain(ks[24], (D,)),
    }


def reference(x, mem, norm_mix, w_in, lambda_q1, lambda_k1, lambda_q2, lambda_k2, diff_subln,
              norm_moba_out, w_out, norm_mem_q, norm_mem_kv, w_mem_q, w_mem_kv, w_mem_o,
              norm_ffn, w_router_group, b_router_group, w_router_expert, b_router_expert,
              w_expert_gate, w_expert_up, w_expert_down, norm_final):
    B, S, _ = x.shape
    split_at = [MOBA_WIDTH, 2 * MOBA_WIDTH, 3 * MOBA_WIDTH,
                3 * MOBA_WIDTH + DIFF_QK_COLS, 3 * MOBA_WIDTH + 2 * DIFF_QK_COLS]
    h = x
    for l in range(DEPTH):
        hn = _rmsnorm(h, norm_mix[l])
        proj = hn @ w_in[l]
        qa, ka, va, qd, kd, vd = jnp.split(proj, split_at, axis=-1)
        a_out = _moba_attention(qa.reshape(B, S, MOBA_HEADS, HEAD_DIM),
                                ka.reshape(B, S, MOBA_HEADS, HEAD_DIM),
                                va.reshape(B, S, MOBA_HEADS, HEAD_DIM))
        a_out = _rmsnorm(a_out, norm_moba_out[l])
        lam_init = _lambda_init(l)
        lam = (jnp.exp(jnp.sum(lambda_q1[l].astype(jnp.float32) * lambda_k1[l].astype(jnp.float32)))
               - jnp.exp(jnp.sum(lambda_q2[l].astype(jnp.float32) * lambda_k2[l].astype(jnp.float32)))
               + lam_init)
        d_out = _diff_attention(qd.reshape(B, S, DIFF_HEADS, 2, DIFF_QK_DIM),
                                kd.reshape(B, S, DIFF_HEADS, 2, DIFF_QK_DIM),
                                vd.reshape(B, S, DIFF_HEADS, DIFF_V_DIM),
                                lam, diff_subln[l], lam_init)
        h = h + jnp.concatenate([a_out, d_out], axis=-1) @ w_out[l]
        h = h + _memory_cross_attention(_rmsnorm(h, norm_mem_q[l]), _rmsnorm(mem, norm_mem_kv[l]),
                                        w_mem_q[l], w_mem_kv[l], w_mem_o[l])
        h = h + _hierarchical_moe(_rmsnorm(h, norm_ffn[l]), w_router_group[l], b_router_group[l],
                                  w_router_expert[l], b_router_expert[l],
                                  w_expert_gate[l], w_expert_up[l], w_expert_down[l])
    return _rmsnorm(h, norm_final)
```

```python
import functools
import math

import jax
import jax.numpy as jnp
import numpy as np
from jax import lax
from jax.experimental import pallas as pl
from jax.experimental.pallas import tpu as pltpu

F32 = jnp.float32
BF16 = jnp.bfloat16

HEAD_DIM = 64
MOBA_HEADS = 8
MOBA_WIDTH = MOBA_HEADS * HEAD_DIM
MOBA_BLOCK = 256
MOBA_TOPK = 3
DIFF_HEADS = 4
DIFF_V_DIM = 2 * HEAD_DIM
DIFF_WIDTH = DIFF_HEADS * DIFF_V_DIM
MEM_HEADS = 4
N_GROUPS = 4
EXPERTS_PER_GROUP = 8
N_EXPERTS = N_GROUPS * EXPERTS_PER_GROUP
RMS_EPS = 1e-6
NEG_INF = -1e30
LANES = 128
ROUTE_COLS = LANES
EXPERT_COL0 = N_GROUPS

TM_PROJ = 512
TQ_ATTN = MOBA_BLOCK
TQ_CROSS = 512
TM_EXPERT = 512
TQ_ROWS = 256
VMEM_LIMIT = 56 * 1024 * 1024


def _alibi_slopes(n):
    return [2.0 ** (-8.0 * (i + 1) / n) for i in range(n)]


def _rms(x, g):
    y = x * lax.rsqrt(jnp.mean(x * x, axis=-1, keepdims=True) + RMS_EPS)
    return y * g


def _dot_nt(a, b):
    return lax.dot_general(a, b, (((1,), (1,)), ((), ())), preferred_element_type=F32)


def _params(*sem):
    return pltpu.CompilerParams(dimension_semantics=sem, vmem_limit_bytes=VMEM_LIMIT)


def _inproj_kernel(x_ref, g_ref, w_ref, *out_refs):
    xn = _rms(x_ref[...], g_ref[...]).astype(BF16)
    width = out_refs[0].shape[1]
    for i, o_ref in enumerate(out_refs):
        o_ref[...] = jnp.dot(xn, w_ref[:, i * width:(i + 1) * width],
                             preferred_element_type=F32).astype(BF16)


def _inproj(x2, g, w):
    t, d = x2.shape
    n_out = w.shape[1] // MOBA_WIDTH
    return pl.pallas_call(
        _inproj_kernel,
        grid=(t // TM_PROJ,),
        in_specs=[pl.BlockSpec((TM_PROJ, d), lambda i: (i, 0)),
                  pl.BlockSpec((1, d), lambda i: (0, 0)),
                  pl.BlockSpec(w.shape, lambda i: (0, 0))],
        out_specs=[pl.BlockSpec((TM_PROJ, MOBA_WIDTH), lambda i: (i, 0))] * n_out,
        out_shape=[jax.ShapeDtypeStruct((t, MOBA_WIDTH), BF16)] * n_out,
        compiler_params=_params("arbitrary"),
        name="inproj",
    )(x2, g, w)


def _softmax_block_update(s, v, state):
    m_prev, l_prev, acc_prev = state
    m_new = jnp.maximum(m_prev, jnp.max(s, axis=-1, keepdims=True))
    alpha = jnp.exp(m_prev - m_new)
    p = jnp.exp(s - m_new)
    l_new = alpha * l_prev + jnp.sum(p, axis=-1, keepdims=True)
    acc_new = alpha * acc_prev + jnp.dot(p.astype(BF16), v, preferred_element_type=F32)
    return m_new, l_new, acc_new


def _softmax_block_first(s, v):
    m = jnp.max(s, axis=-1, keepdims=True)
    p = jnp.exp(s - m)
    l = jnp.sum(p, axis=-1, keepdims=True)
    acc = jnp.dot(p.astype(BF16), v, preferred_element_type=F32)
    return m, l, acc


def _causal_attn(qh, k_ref, v_ref, cols_k, cols_v, j, slope, causal, key_pos, row_sel):
    blk = MOBA_BLOCK
    own = pl.ds(pl.multiple_of(j * blk, blk), blk)
    s = _dot_nt(qh, k_ref[own, cols_k]) + slope * key_pos
    s = jnp.where(causal, s, NEG_INF)
    state = _softmax_block_first(s, v_ref[own, cols_v])

    def body(n, st):
        rows = pl.ds(pl.multiple_of(n * blk, blk), blk)
        bias = slope * key_pos - (slope * blk) * (j - n).astype(F32)
        sn = _dot_nt(qh, k_ref[rows, cols_k]) + bias
        sel = row_sel(n)
        if sel is not None:
            sn = jnp.where(sel, sn, NEG_INF)
        return _softmax_block_update(sn, v_ref[rows, cols_v], st)

    _, l, acc = lax.fori_loop(0, j, body, state)
    return acc / l


def _moba_kernel(q_ref, k_ref, v_ref, g_ref, o_ref, kmean_ref):
    j = pl.program_id(1)
    blk = MOBA_BLOCK
    nblk = kmean_ref.shape[0]
    slopes = _alibi_slopes(MOBA_HEADS)

    @pl.when(j == 0)
    def _():
        for n in range(nblk):
            kb = k_ref[n * blk:(n + 1) * blk, :].astype(F32)
            kmean_ref[n:n + 1, :] = jnp.sum(kb, axis=0, keepdims=True) * (1.0 / blk)

    lane = lax.broadcasted_iota(jnp.int32, (blk, LANES), 1)
    qi = lax.broadcasted_iota(jnp.int32, (blk, blk), 0)
    ki = lax.broadcasted_iota(jnp.int32, (blk, blk), 1)
    causal = qi >= ki
    key_pos = lax.broadcasted_iota(jnp.int32, (1, blk), 1).astype(F32)
    blk_id = lax.broadcasted_iota(jnp.int32, (blk, nblk), 1)

    slabs = []
    for p in range(MOBA_WIDTH // LANES):
        cols = slice(p * LANES, (p + 1) * LANES)
        q_slab = q_ref[:, cols]
        km = kmean_ref[:, cols]
        outs = []
        for half in range(2):
            h = 2 * p + half
            in_head = (lane >= half * HEAD_DIM) & (lane < (half + 1) * HEAD_DIM)
            gate = lax.dot_general(jnp.where(in_head, q_slab, 0).astype(F32), km,
                                   (((1,), (1,)), ((), ())), precision=lax.Precision.HIGHEST,
                                   preferred_element_type=F32)
            rank = jnp.zeros((blk, nblk), jnp.int32)
            for m in range(nblk):
                gm = gate[:, m:m + 1]
                ahead = (gm > gate) | ((gm == gate) & (m < blk_id))
                rank = rank + jnp.where(ahead & (m < j), 1, 0)
            selected = jnp.where((blk_id < j) & (rank < MOBA_TOPK), 1.0, 0.0)

            def row_sel(n, selected=selected):
                return jnp.sum(jnp.where(blk_id == n, selected, 0.0), axis=1, keepdims=True) > 0.5

            qh = jnp.where(in_head, q_slab * (HEAD_DIM ** -0.5), 0).astype(BF16)
            outs.append(_causal_attn(qh, k_ref, v_ref, cols, cols, j, slopes[h], causal, key_pos, row_sel))
        slabs.append(jnp.where(lane < HEAD_DIM, outs[0], outs[1]))
    o = jnp.concatenate(slabs, axis=1)
    o_ref[...] = _rms(o, g_ref[...]).astype(BF16)


def _moba(qa, ka, va, g, batch, seq):
    nblk = seq // MOBA_BLOCK
    w = MOBA_WIDTH
    return pl.pallas_call(
        _moba_kernel,
        grid=(batch, nblk),
        in_specs=[pl.BlockSpec((MOBA_BLOCK, w), lambda b, j: (b * nblk + j, 0)),
                  pl.BlockSpec((seq, w), lambda b, j: (b, 0)),
                  pl.BlockSpec((seq, w), lambda b, j: (b, 0)),
                  pl.BlockSpec((1, w), lambda b, j: (0, 0))],
        out_specs=pl.BlockSpec((MOBA_BLOCK, w), lambda b, j: (b * nblk + j, 0)),
        out_shape=jax.ShapeDtypeStruct((batch * seq, w), BF16),
        scratch_shapes=[pltpu.VMEM((nblk, w), F32)],
        compiler_params=_params("arbitrary", "arbitrary"),
        name="moba",
    )(qa, ka, va, g)


def _diff_kernel(q_ref, k_ref, v_ref, lam_ref, g_ref, o_ref, *, lam_init):
    j = pl.program_id(1)
    blk = MOBA_BLOCK
    slopes = _alibi_slopes(DIFF_HEADS)
    lane = lax.broadcasted_iota(jnp.int32, (blk, LANES), 1)
    qi = lax.broadcasted_iota(jnp.int32, (blk, blk), 0)
    ki = lax.broadcasted_iota(jnp.int32, (blk, blk), 1)
    causal = qi >= ki
    key_pos = lax.broadcasted_iota(jnp.int32, (1, blk), 1).astype(F32)

    lv = lam_ref[...]
    lam = (jnp.exp(jnp.sum(lv[0:1] * lv[1:2], axis=1, keepdims=True))
           - jnp.exp(jnp.sum(lv[2:3] * lv[3:4], axis=1, keepdims=True)) + lam_init)

    for h in range(DIFF_HEADS):
        cols = slice(h * LANES, (h + 1) * LANES)
        q_slab = q_ref[:, cols]
        maps = []
        for c in range(2):
            in_map = (lane >= c * HEAD_DIM) & (lane < (c + 1) * HEAD_DIM)
            qh = jnp.where(in_map, q_slab * (HEAD_DIM ** -0.5), 0).astype(BF16)
            maps.append(_causal_attn(qh, k_ref, v_ref, cols, cols, j, slopes[h], causal, key_pos,
                                     lambda n: None))
        o = maps[0] - lam * maps[1]
        o_ref[:, cols] = (_rms(o, g_ref[...]) * (1.0 - lam_init)).astype(BF16)


def _diff(qd, kd, vd, lams, g, batch, seq, lam_init):
    nblk = seq // MOBA_BLOCK
    w = DIFF_WIDTH
    return pl.pallas_call(
        functools.partial(_diff_kernel, lam_init=lam_init),
        grid=(batch, nblk),
        in_specs=[pl.BlockSpec((MOBA_BLOCK, w), lambda b, j: (b * nblk + j, 0)),
                  pl.BlockSpec((seq, w), lambda b, j: (b, 0)),
                  pl.BlockSpec((seq, w), lambda b, j: (b, 0)),
                  pl.BlockSpec(lams.shape, lambda b, j: (0, 0)),
                  pl.BlockSpec((1, DIFF_V_DIM), lambda b, j: (0, 0))],
        out_specs=pl.BlockSpec((MOBA_BLOCK, w), lambda b, j: (b * nblk + j, 0)),
        out_shape=jax.ShapeDtypeStruct((batch * seq, w), BF16),
        compiler_params=_params("arbitrary", "arbitrary"),
        name="diff",
    )(qd, kd, vd, lams, g)


def _outproj_kernel(a_ref, d_ref, x_ref, wo_ref, g_ref, wq_ref, h_ref, q_ref, *, q_scale):
    wa = a_ref.shape[1]
    h = (x_ref[...]
         + jnp.dot(a_ref[...], wo_ref[:wa, :], preferred_element_type=F32)
         + jnp.dot(d_ref[...], wo_ref[wa:, :], preferred_element_type=F32))
    h_ref[...] = h
    qn = _rms(h, g_ref[...]).astype(BF16)
    q_ref[...] = (jnp.dot(qn, wq_ref[...], preferred_element_type=F32) * q_scale).astype(BF16)


def _outproj(a, d, x2, w_out, g, w_q, q_scale):
    t, dm = x2.shape
    tile = lambda w: pl.BlockSpec((TM_PROJ, w), lambda i: (i, 0))
    whole = lambda arr: pl.BlockSpec(arr.shape, lambda i: (0, 0))
    return pl.pallas_call(
        functools.partial(_outproj_kernel, q_scale=q_scale),
        grid=(t // TM_PROJ,),
        in_specs=[tile(a.shape[1]), tile(d.shape[1]), tile(dm), whole(w_out), whole(g), whole(w_q)],
        out_specs=[tile(dm), tile(dm)],
        out_shape=[jax.ShapeDtypeStruct((t, dm), F32), jax.ShapeDtypeStruct((t, dm), BF16)],
        compiler_params=_params("arbitrary"),
        name="outproj",
    )(a, d, x2, w_out, g, w_q)


def _memkv_kernel(m_ref, g_ref, w_ref, kv_ref):
    mn = _rms(m_ref[...], g_ref[...]).astype(BF16)
    kv_ref[...] = jnp.dot(mn, w_ref[...], preferred_element_type=F32).astype(BF16)


def _memkv(mem2, g, w_kv, mem_len):
    rows, dm = mem2.shape
    return pl.pallas_call(
        _memkv_kernel,
        grid=(rows // mem_len,),
        in_specs=[pl.BlockSpec((mem_len, dm), lambda i: (i, 0)),
                  pl.BlockSpec((1, dm), lambda i: (0, 0)),
                  pl.BlockSpec(w_kv.shape, lambda i: (0, 0))],
        out_specs=pl.BlockSpec((mem_len, w_kv.shape[1]), lambda i: (i, 0)),
        out_shape=jax.ShapeDtypeStruct((rows, w_kv.shape[1]), BF16),
        compiler_params=_params("arbitrary"),
        name="memkv",
    )(mem2, g, w_kv)


def _cross_kernel(q_ref, kv_ref, h_ref, wo_ref, g_ref, wr_ref, br_ref,
                  h2_ref, xn_ref, route_ref, counts_ref, run_ref):
    first = (pl.program_id(0) == 0) & (pl.program_id(1) == 0)

    @pl.when(first)
    def _():
        run_ref[...] = jnp.zeros_like(run_ref)

    tq, dm = h_ref.shape
    dh = dm // MEM_HEADS
    heads = []
    for h in range(MEM_HEADS):
        s = _dot_nt(q_ref[:, h * dh:(h + 1) * dh], kv_ref[:, h * dh:(h + 1) * dh])
        m = jnp.max(s, axis=-1, keepdims=True)
        e = jnp.exp(s - m)
        l = jnp.sum(e, axis=-1, keepdims=True)
        o = jnp.dot(e.astype(BF16), kv_ref[:, dm + h * dh:dm + (h + 1) * dh], preferred_element_type=F32)
        heads.append((o / l).astype(BF16))
    o = jnp.concatenate(heads, axis=1)
    h2 = h_ref[...] + jnp.dot(o, wo_ref[...], preferred_element_type=F32)
    h2_ref[...] = h2
    xn = _rms(h2, g_ref[...])
    xn_ref[...] = xn

    logits = jnp.dot(xn, wr_ref[...], precision=lax.Precision.HIGHEST, preferred_element_type=F32) + br_ref[...]
    lane = lax.broadcasted_iota(jnp.int32, logits.shape, 1)
    big = jnp.int32(ROUTE_COLS)

    def top1(vals):
        v = jnp.max(vals, axis=-1, keepdims=True)
        i = jnp.min(jnp.where(vals == v, lane, big), axis=-1, keepdims=True)
        return v, i

    gl = jnp.where(lane < N_GROUPS, logits, -jnp.inf)
    g_max, g_idx = top1(gl)
    g_w = 1.0 / jnp.sum(jnp.exp(gl - g_max), axis=-1, keepdims=True)
    lo = EXPERT_COL0 + g_idx * EXPERTS_PER_GROUP
    el = jnp.where((lane >= lo) & (lane < lo + EXPERTS_PER_GROUP), logits, -jnp.inf)
    v1, i1 = top1(el)
    v2, i2 = top1(jnp.where(lane == i1, -jnp.inf, el))
    e2 = jnp.exp(v2 - v1)
    w1 = g_w * (1.0 / (1.0 + e2))
    w2 = g_w * (e2 / (1.0 + e2))

    onehot = jnp.where((lane == i1) | (lane == i2), 1.0, 0.0)
    ri = lax.broadcasted_iota(jnp.int32, (tq, tq), 0)
    ci = lax.broadcasted_iota(jnp.int32, (tq, tq), 1)
    earlier = jnp.where(ci < ri, 1.0, 0.0).astype(BF16)
    before = jnp.dot(earlier, onehot.astype(BF16), preferred_element_type=F32) + run_ref[...]
    r1 = jnp.sum(jnp.where(lane == i1, before, 0.0), axis=-1, keepdims=True)
    r2 = jnp.sum(jnp.where(lane == i2, before, 0.0), axis=-1, keepdims=True)
    run_ref[...] = run_ref[...] + jnp.sum(onehot, axis=0, keepdims=True)
    counts_ref[...] = run_ref[...]

    rec = jnp.zeros(logits.shape, F32)
    for col, val in enumerate(((i1 - EXPERT_COL0).astype(F32), (i2 - EXPERT_COL0).astype(F32), w1, w2, r1, r2)):
        rec = jnp.where(lane == col, val, rec)
    route_ref[...] = rec


def _cross(qc, kv, h1, w_o, g, w_r, b_r, batch, seq, mem_len):
    t, dm = h1.shape
    nt = seq // TQ_CROSS
    tile = lambda w: pl.BlockSpec((TQ_CROSS, w), lambda b, i: (b * nt + i, 0))
    whole = lambda arr: pl.BlockSpec(arr.shape, lambda b, i: (0, 0))
    return pl.pallas_call(
        _cross_kernel,
        grid=(batch, nt),
        in_specs=[tile(dm), pl.BlockSpec((mem_len, kv.shape[1]), lambda b, i: (b, 0)), tile(dm),
                  whole(w_o), whole(g), whole(w_r), whole(b_r)],
        out_specs=[tile(dm), tile(dm), tile(ROUTE_COLS), pl.BlockSpec((1, ROUTE_COLS), lambda b, i: (0, 0))],
        out_shape=[jax.ShapeDtypeStruct((t, dm), F32), jax.ShapeDtypeStruct((t, dm), F32),
                   jax.ShapeDtypeStruct((t, ROUTE_COLS), F32), jax.ShapeDtypeStruct((1, ROUTE_COLS), F32)],
        scratch_shapes=[pltpu.VMEM((1, ROUTE_COLS), F32)],
        compiler_params=_params("arbitrary", "arbitrary"),
        name="cross",
    )(qc, kv, h1, w_o, g, w_r, b_r)


def _row_copy(src_ref, src_row, dst_ref, dst_row, sem):
    return pltpu.make_async_copy(src_ref.at[pl.ds(src_row, 1)], dst_ref.at[pl.ds(dst_row, 1)], sem)


def _dispatch_kernel(pos_ref, xn_ref, xs_in_ref, xs_ref, sem):
    del xs_in_ref
    i = pl.program_id(0)
    tq = xn_ref.shape[0]
    base = i * (2 * tq)

    def issue(t, c):
        _row_copy(xn_ref, t, xs_ref, pos_ref[base + 2 * t], sem).start()
        _row_copy(xn_ref, t, xs_ref, pos_ref[base + 2 * t + 1], sem).start()
        return c

    lax.fori_loop(0, tq, issue, 0)

    def drain(t, c):
        _row_copy(xn_ref, 0, xs_ref, 0, sem).wait()
        _row_copy(xn_ref, 0, xs_ref, 0, sem).wait()
        return c

    lax.fori_loop(0, tq, drain, 0)


def _dispatch(pos, xn, xs_zero):
    t, dm = xn.shape
    return pl.pallas_call(
        _dispatch_kernel,
        grid_spec=pltpu.PrefetchScalarGridSpec(
            num_scalar_prefetch=1,
            grid=(t // TQ_ROWS,),
            in_specs=[pl.BlockSpec((TQ_ROWS, dm), lambda i, pos: (i, 0)),
                      pl.BlockSpec(memory_space=pl.ANY)],
            out_specs=pl.BlockSpec(memory_space=pl.ANY),
            scratch_shapes=[pltpu.SemaphoreType.DMA]),
        out_shape=jax.ShapeDtypeStruct(xs_zero.shape, xs_zero.dtype),
        input_output_aliases={2: 0},
        compiler_params=_params("arbitrary"),
        name="dispatch",
    )(pos, xn, xs_zero)


def _experts_kernel(te_ref, nused_ref, xs_ref, wgu_ref, wd_ref, y_ref):
    i = pl.program_id(0)
    ff = wd_ref.shape[0]

    @pl.when(i < nused_ref[0])
    def _():
        gu = jnp.dot(xs_ref[...].astype(BF16), wgu_ref[...], preferred_element_type=F32)
        g = gu[:, :ff]
        hh = (g * jax.nn.sigmoid(g)) * gu[:, ff:]
        y_ref[...] = jnp.dot(hh.astype(BF16), wd_ref[...], preferred_element_type=F32)

    @pl.when(i >= nused_ref[0])
    def _():
        y_ref[...] = jnp.zeros_like(y_ref)


def _experts(tile_expert, n_used, xs, wgu, wd):
    rows, dm = xs.shape
    ff = wd.shape[1]
    return pl.pallas_call(
        _experts_kernel,
        grid_spec=pltpu.PrefetchScalarGridSpec(
            num_scalar_prefetch=2,
            grid=(rows // TM_EXPERT,),
            in_specs=[pl.BlockSpec((TM_EXPERT, dm), lambda i, te, nu: (i, 0)),
                      pl.BlockSpec((None, dm, 2 * ff), lambda i, te, nu: (te[i], 0, 0)),
                      pl.BlockSpec((None, ff, dm), lambda i, te, nu: (te[i], 0, 0))],
            out_specs=pl.BlockSpec((TM_EXPERT, dm), lambda i, te, nu: (i, 0))),
        out_shape=jax.ShapeDtypeStruct((rows, dm), F32),
        compiler_params=_params("arbitrary"),
        name="experts",
    )(tile_expert, n_used, xs, wgu, wd)


def _combine_kernel(pos_ref, h2_ref, route_ref, g_ref, y_ref, o_ref, ybuf, sem):
    i = pl.program_id(0)
    tq = h2_ref.shape[0]
    base = i * (2 * tq)

    def issue(t, c):
        _row_copy(y_ref, pos_ref[base + 2 * t], ybuf, t, sem).start()
        _row_copy(y_ref, pos_ref[base + 2 * t + 1], ybuf, tq + t, sem).start()
        return c

    lax.fori_loop(0, tq, issue, 0)

    def drain(t, c):
        _row_copy(y_ref, 0, ybuf, 0, sem).wait()
        _row_copy(y_ref, 0, ybuf, 0, sem).wait()
        return c

    lax.fori_loop(0, tq, drain, 0)

    rec = route_ref[...]
    h = h2_ref[...] + rec[:, 2:3] * ybuf[:tq, :] + rec[:, 3:4] * ybuf[tq:, :]
    o_ref[...] = _rms(h, g_ref[...])


def _combine(pos, h2, route, g, y):
    t, dm = h2.shape
    return pl.pallas_call(
        _combine_kernel,
        grid_spec=pltpu.PrefetchScalarGridSpec(
            num_scalar_prefetch=1,
            grid=(t // TQ_ROWS,),
            in_specs=[pl.BlockSpec((TQ_ROWS, dm), lambda i, pos: (i, 0)),
                      pl.BlockSpec((TQ_ROWS, ROUTE_COLS), lambda i, pos: (i, 0)),
                      pl.BlockSpec((1, dm), lambda i, pos: (0, 0)),
                      pl.BlockSpec(memory_space=pl.ANY)],
            out_specs=pl.BlockSpec((TQ_ROWS, dm), lambda i, pos: (i, 0)),
            scratch_shapes=[pltpu.VMEM((2 * TQ_ROWS, dm), F32), pltpu.SemaphoreType.DMA]),
        out_shape=jax.ShapeDtypeStruct((t, dm), F32),
        compiler_params=_params("arbitrary"),
        name="combine",
    )(pos, h2, route, g, y)


def kernel(x, mem, norm_mix, w_in, lambda_q1, lambda_k1, lambda_q2, lambda_k2, diff_subln, norm_moba_out, w_out, norm_mem_q, norm_mem_kv, w_mem_q, w_mem_kv, w_mem_o, norm_ffn, w_router_group, b_router_group, w_router_expert, b_router_expert, w_expert_gate, w_expert_up, w_expert_down, norm_final):
    batch, seq, dm = x.shape
    mem_len = mem.shape[1]
    depth = w_in.shape[0]
    t = batch * seq
    assert seq % MOBA_BLOCK == 0 and seq % TQ_CROSS == 0 and t % TM_PROJ == 0

    h = x.reshape(t, dm)
    mem2 = mem.reshape(batch * mem_len, dm)
    row = lambda v: v.reshape(1, -1).astype(F32)
    for l in range(depth):
        lam_init = 0.8 - 0.6 * math.exp(-0.3 * l)
        qa, ka, va, qd, kd, vd = _inproj(h, row(norm_mix[l]), w_in[l].astype(BF16))
        a = _moba(qa, ka, va, row(norm_moba_out[l]), batch, seq)
        lams = jnp.stack([lambda_q1[l], lambda_k1[l], lambda_q2[l], lambda_k2[l]]).astype(F32)
        d = _diff(qd, kd, vd, lams, row(diff_subln[l]), batch, seq, lam_init)
        h1, qc = _outproj(a, d, h, w_out[l].astype(BF16), row(norm_mem_q[l]), w_mem_q[l].astype(BF16),
                          (dm // MEM_HEADS) ** -0.5)
        kv = _memkv(mem2, row(norm_mem_kv[l]), w_mem_kv[l].astype(BF16), mem_len)

        pad = ROUTE_COLS - N_GROUPS - N_EXPERTS
        w_r = jnp.pad(jnp.concatenate([w_router_group[l], w_router_expert[l]], axis=1).astype(F32), ((0, 0), (0, pad)))
        b_r = jnp.pad(jnp.concatenate([b_router_group[l], b_router_expert[l]]).astype(F32), (0, pad)).reshape(1, -1)
        h2, xn, route, counts = _cross(qc, kv, h1, w_mem_o[l].astype(BF16), row(norm_ffn[l]), w_r, b_r,
                                       batch, seq, mem_len)

        tm = TM_EXPERT
        n_tiles = (2 * t) // tm + N_EXPERTS
        cnt = counts[0, EXPERT_COL0:EXPERT_COL0 + N_EXPERTS].astype(jnp.int32)
        padded = ((cnt + tm - 1) // tm) * tm
        ends = jnp.cumsum(padded)
        offs = ends - padded
        eidx = route[:, 0:2].astype(jnp.int32)
        pos = (offs[eidx] + route[:, 4:6].astype(jnp.int32)).reshape(2 * t)
        tile_expert = jnp.minimum(jnp.searchsorted(ends, jnp.arange(n_tiles, dtype=jnp.int32) * tm, side="right"),
                                  N_EXPERTS - 1).astype(jnp.int32)
        n_used = (ends[-1:] // tm).astype(jnp.int32)

        xs = _dispatch(pos, xn, jnp.zeros((n_tiles * tm, dm), F32))
        wgu = jnp.concatenate([w_expert_gate[l], w_expert_up[l]], axis=2).astype(BF16)
        y = _experts(tile_expert, n_used, xs, wgu, w_expert_down[l].astype(BF16))
        assert depth == 1
        h = _combine(pos, h2, route, row(norm_final), y)
    return h.reshape(batch, seq, dm)
```

```python
import functools
import math

import jax
import jax.numpy as jnp
import numpy as np
from jax import lax
from jax.experimental import pallas as pl
from jax.experimental.pallas import tpu as pltpu

F32 = jnp.float32
BF16 = jnp.bfloat16

HEAD_DIM = 64
MOBA_HEADS = 8
MOBA_WIDTH = MOBA_HEADS * HEAD_DIM
MOBA_BLOCK = 256
MOBA_TOPK = 3
DIFF_HEADS = 4
DIFF_V_DIM = 2 * HEAD_DIM
DIFF_WIDTH = DIFF_HEADS * DIFF_V_DIM
MEM_HEADS = 4
N_GROUPS = 4
EXPERTS_PER_GROUP = 8
N_EXPERTS = N_GROUPS * EXPERTS_PER_GROUP
RMS_EPS = 1e-6
NEG_INF = -1e30
LOG2E = math.log2(math.e)
LANES = 128
ROUTE_COLS = LANES
EXPERT_COL0 = N_GROUPS

TM_PROJ = 512
TQ_ATTN = MOBA_BLOCK
TQ_CROSS = 512
TM_EXPERT = 512
TQ_ROWS = 256
VMEM_LIMIT = 56 * 1024 * 1024


def _alibi_slopes(n):
    return [2.0 ** (-8.0 * (i + 1) / n) for i in range(n)]


def _rms(x, g):
    y = x * lax.rsqrt(jnp.mean(x * x, axis=-1, keepdims=True) + RMS_EPS)
    return y * g


def _dot_nt(a, b):
    return lax.dot_general(a, b, (((1,), (1,)), ((), ())), preferred_element_type=F32)


def _params(*sem):
    return pltpu.CompilerParams(dimension_semantics=sem, vmem_limit_bytes=VMEM_LIMIT)


def _inproj_kernel(x_ref, g_ref, w_ref, *out_refs, scales):
    xn = _rms(x_ref[...], g_ref[...]).astype(BF16)
    width = out_refs[0].shape[1]
    for i, o_ref in enumerate(out_refs):
        y = jnp.dot(xn, w_ref[:, i * width:(i + 1) * width], preferred_element_type=F32)
        o_ref[...] = (y if scales[i] == 1.0 else y * scales[i]).astype(BF16)


def _inproj(x2, g, w, scales):
    t, d = x2.shape
    n_out = w.shape[1] // MOBA_WIDTH
    return pl.pallas_call(
        functools.partial(_inproj_kernel, scales=scales),
        grid=(t // TM_PROJ,),
        in_specs=[pl.BlockSpec((TM_PROJ, d), lambda i: (i, 0)),
                  pl.BlockSpec((1, d), lambda i: (0, 0)),
                  pl.BlockSpec(w.shape, lambda i: (0, 0))],
        out_specs=[pl.BlockSpec((TM_PROJ, MOBA_WIDTH), lambda i: (i, 0))] * n_out,
        out_shape=[jax.ShapeDtypeStruct((t, MOBA_WIDTH), BF16)] * n_out,
        compiler_params=_params("arbitrary"),
        name="inproj",
    )(x2, g, w)


AUX_PEN0 = 0
AUX_BLK0 = 64
AUX_OFF0 = 68
AUX_ROW = 72
N_SPLIT = 4


def _bf16_pieces(c):
    pieces, rem = [], np.float64(c)
    for _ in range(N_SPLIT):
        p = np.float64(np.asarray(rem).astype(BF16))
        pieces.append(p)
        rem = rem - p
    return pieces


def _key_aux(seq):
    pos = np.arange(seq)
    blk, off = pos // MOBA_BLOCK, pos % MOBA_BLOCK
    assert blk.max() < 8, "penalty lanes hold 8 blocks per head"
    a = np.zeros((seq, LANES), np.float32)
    for h in range(MOBA_HEADS):
        a[pos, AUX_PEN0 + h * 8 + blk] = 1.0
    a[:, AUX_BLK0:AUX_BLK0 + N_SPLIT] = (blk * MOBA_BLOCK)[:, None]
    a[:, AUX_OFF0:AUX_OFF0 + N_SPLIT] = off[:, None]
    a[:, AUX_ROW] = 1.0
    return jnp.asarray(a, BF16)


def _value_aux():
    a = np.zeros((MOBA_BLOCK, LANES), np.float32)
    a[:, 0] = 1.0
    return jnp.asarray(a, BF16)


def _query_aux_rows(slopes):
    a = np.zeros((len(slopes), LANES), np.float32)
    for i, s in enumerate(slopes):
        a[i, AUX_BLK0:AUX_BLK0 + N_SPLIT] = _bf16_pieces(s * LOG2E)
        a[i, AUX_OFF0:AUX_OFF0 + N_SPLIT] = _bf16_pieces(s * LOG2E)
    return jnp.asarray(a)


def _query_aux(crow, slope, j, pen_t=None, pen_lo=0):
    blk = MOBA_BLOCK
    lane = lax.broadcasted_iota(jnp.int32, (blk, LANES), 1)
    t_q = (j * blk).astype(F32) + lax.broadcasted_iota(jnp.int32, (blk, LANES), 0).astype(F32)
    aux = jnp.where(lane == AUX_ROW, (-slope * LOG2E) * t_q, crow)
    if pen_t is not None:
        aux = jnp.where((lane >= pen_lo) & (lane < pen_lo + 8), pen_t, aux)
    return aux.astype(BF16)


def _attend(qaug_ref, k_ref, v_ref, kaux_ref, vaux_ref, s_ref, sown_ref, mrun_ref, acc_ref, j, kcols, vcols):
    n_items = qaug_ref.shape[0]
    blk = MOBA_BLOCK
    causal = (lax.broadcasted_iota(jnp.int32, (blk, blk), 0) >= lax.broadcasted_iota(jnp.int32, (blk, blk), 1))
    own = pl.ds(pl.multiple_of(j * blk, blk), blk)
    block_rows = lambda n: pl.ds(pl.multiple_of(n * blk, blk), blk)

    def scores(i, rows, kaux):
        return _dot_nt(qaug_ref[i], jnp.concatenate([k_ref[rows, kcols[i]], kaux], axis=1))

    def half_max(s):
        return jnp.maximum(s[:, :LANES], s[:, LANES:])

    kaux_own = kaux_ref[own, :]
    for i in range(n_items):
        s = jnp.where(causal, scores(i, own, kaux_own), NEG_INF)
        sown_ref[i] = s
        mrun_ref[i] = half_max(s)

    def pass_a(n, c):
        kaux_n = kaux_ref[block_rows(n), :]
        for i in range(n_items):
            s = scores(i, block_rows(n), kaux_n)
            s_ref[i, n] = s
            mrun_ref[i] = jnp.maximum(mrun_ref[i], half_max(s))
        return c

    lax.fori_loop(0, j, pass_a, 0)

    for i in range(n_items):
        mrun_ref[i] = jnp.broadcast_to(jnp.max(mrun_ref[i], axis=1, keepdims=True), (blk, LANES))

    vaux = vaux_ref[...]

    def pv(i, s, rows):
        m = mrun_ref[i]
        p = jnp.concatenate([jnp.exp2(s[:, :LANES] - m), jnp.exp2(s[:, LANES:] - m)], axis=1).astype(BF16)
        return jnp.dot(p, jnp.concatenate([v_ref[rows, vcols[i]], vaux], axis=1), preferred_element_type=F32)

    for i in range(n_items):
        acc_ref[i] = pv(i, sown_ref[i], own)

    def pass_b(n, c):
        for i in range(n_items):
            acc_ref[i] += pv(i, s_ref[i, n], block_rows(n))
        return c

    lax.fori_loop(0, j, pass_b, 0)

    outs = []
    for i in range(n_items):
        a = acc_ref[i]
        outs.append(a[:, :LANES] / a[:, LANES:LANES + 1])
    return outs


def _attend_scratch(n_items, nblk):
    blk = MOBA_BLOCK
    return [pltpu.VMEM((n_items, blk, 2 * LANES), BF16),
            pltpu.VMEM((n_items, nblk, blk, blk), F32),
            pltpu.VMEM((n_items, blk, blk), F32),
            pltpu.VMEM((n_items, blk, LANES), F32),
            pltpu.VMEM((n_items, blk, 2 * LANES), F32)]


def _moba_kernel(q_ref, k_ref, v_ref, kaux_ref, vaux_ref, crow_ref, g_ref, o_ref,
                 kmean_ref, kbd_ref, qaug_ref, s_ref, sown_ref, mrun_ref, acc_ref):
    j = pl.program_id(1)
    blk = MOBA_BLOCK
    nblk = s_ref.shape[1]
    slopes = _alibi_slopes(MOBA_HEADS)
    n_slabs = MOBA_WIDTH // LANES

    @pl.when(j == 0)
    def _():
        kmean_ref[...] = jnp.zeros_like(kmean_ref)
        for n in range(nblk):
            kb = k_ref[n * blk:(n + 1) * blk, :].astype(F32)
            kmean_ref[n:n + 1, :] = jnp.sum(kb, axis=0, keepdims=True) * (1.0 / blk)
        col_head = lax.broadcasted_iota(jnp.int32, kmean_ref.shape, 1) // HEAD_DIM
        for h in range(MOBA_HEADS):
            kbd_ref[h * 8:(h + 1) * 8, :] = jnp.where(col_head == h, kmean_ref[...], 0.0)

    gate = lax.dot_general(kbd_ref[...], q_ref[...].astype(F32), (((1,), (1,)), ((), ())),
                           precision=lax.Precision.HIGHEST, preferred_element_type=F32)
    n_iota = lax.broadcasted_iota(jnp.int32, (8, blk), 0)
    pens = []
    for h in range(MOBA_HEADS):
        g = gate[h * 8:(h + 1) * 8, :]
        rank = jnp.zeros((8, blk), jnp.int32)
        for m in range(nblk):
            gm = g[m:m + 1, :]
            ahead = (gm > g) | ((gm == g) & (m < n_iota))
            rank = rank + jnp.where(ahead, 1, 0) * (m < j).astype(jnp.int32)
        pens.append(jnp.where((n_iota < j) & (rank >= MOBA_TOPK), NEG_INF, 0.0))
    pen_t = jnp.concatenate(pens + [jnp.zeros((LANES - 8 * MOBA_HEADS, blk), F32)], axis=0).T

    lane = lax.broadcasted_iota(jnp.int32, (blk, LANES), 1)
    for h in range(MOBA_HEADS):
        half = h % 2
        q_slab = q_ref[:, (h // 2) * LANES:(h // 2 + 1) * LANES]
        in_head = (lane >= half * HEAD_DIM) & (lane < (half + 1) * HEAD_DIM)
        aux = _query_aux(crow_ref[h:h + 1, :], slopes[h], j, pen_t, AUX_PEN0 + h * 8)
        qaug_ref[h] = jnp.concatenate([jnp.where(in_head, q_slab, 0).astype(BF16), aux], axis=1)

    cols = [slice((h // 2) * LANES, (h // 2 + 1) * LANES) for h in range(MOBA_HEADS)]
    outs = _attend(qaug_ref, k_ref, v_ref, kaux_ref, vaux_ref, s_ref, sown_ref, mrun_ref, acc_ref, j, cols, cols)
    o = jnp.concatenate([jnp.where(lane < HEAD_DIM, outs[2 * p], outs[2 * p + 1]) for p in range(n_slabs)], axis=1)
    o_ref[...] = _rms(o, g_ref[...]).astype(BF16)


def _moba(qa, ka, va, key_aux, val_aux, g, batch, seq):
    nblk = seq // MOBA_BLOCK
    w = MOBA_WIDTH
    crow = _query_aux_rows(_alibi_slopes(MOBA_HEADS))
    whole = lambda arr: pl.BlockSpec(arr.shape, lambda b, j: (0, 0))
    return pl.pallas_call(
        _moba_kernel,
        grid=(batch, nblk),
        in_specs=[pl.BlockSpec((MOBA_BLOCK, w), lambda b, j: (b * nblk + j, 0)),
                  pl.BlockSpec((seq, w), lambda b, j: (b, 0)),
                  pl.BlockSpec((seq, w), lambda b, j: (b, 0)),
                  whole(key_aux), whole(val_aux), whole(crow), whole(g)],
        out_specs=pl.BlockSpec((MOBA_BLOCK, w), lambda b, j: (b * nblk + j, 0)),
        out_shape=jax.ShapeDtypeStruct((batch * seq, w), BF16),
        scratch_shapes=[pltpu.VMEM((8, w), F32), pltpu.VMEM((8 * MOBA_HEADS, w), F32)]
        + _attend_scratch(MOBA_HEADS, nblk),
        compiler_params=_params("arbitrary", "arbitrary"),
        name="moba",
    )(qa, ka, va, key_aux, val_aux, crow, g)


def _diff_kernel(q_ref, k_ref, v_ref, kaux_ref, vaux_ref, crow_ref, lam_ref, g_ref, o_ref,
                 qaug_ref, s_ref, sown_ref, mrun_ref, acc_ref, *, lam_init):
    j = pl.program_id(1)
    blk = MOBA_BLOCK
    slopes = _alibi_slopes(DIFF_HEADS)
    lane = lax.broadcasted_iota(jnp.int32, (blk, LANES), 1)

    lv = lam_ref[...]
    lam = (jnp.exp(jnp.sum(lv[0:1] * lv[1:2], axis=1, keepdims=True))
           - jnp.exp(jnp.sum(lv[2:3] * lv[3:4], axis=1, keepdims=True)) + lam_init)

    for i in range(2 * DIFF_HEADS):
        h, c = i // 2, i % 2
        q_slab = q_ref[:, h * LANES:(h + 1) * LANES]
        in_map = (lane >= c * HEAD_DIM) & (lane < (c + 1) * HEAD_DIM)
        aux = _query_aux(crow_ref[i:i + 1, :], slopes[h], j)
        qaug_ref[i] = jnp.concatenate([jnp.where(in_map, q_slab, 0).astype(BF16), aux], axis=1)

    cols = [slice((i // 2) * LANES, (i // 2 + 1) * LANES) for i in range(2 * DIFF_HEADS)]
    outs = _attend(qaug_ref, k_ref, v_ref, kaux_ref, vaux_ref, s_ref, sown_ref, mrun_ref, acc_ref, j, cols, cols)
    for h in range(DIFF_HEADS):
        o = outs[2 * h] - lam * outs[2 * h + 1]
        o_ref[:, h * LANES:(h + 1) * LANES] = (_rms(o, g_ref[...]) * (1.0 - lam_init)).astype(BF16)


def _diff(qd, kd, vd, key_aux, val_aux, lams, g, batch, seq, lam_init):
    nblk = seq // MOBA_BLOCK
    w = DIFF_WIDTH
    crow = _query_aux_rows([s for s in _alibi_slopes(DIFF_HEADS) for _ in range(2)])
    whole = lambda arr: pl.BlockSpec(arr.shape, lambda b, j: (0, 0))
    return pl.pallas_call(
        functools.partial(_diff_kernel, lam_init=lam_init),
        grid=(batch, nblk),
        in_specs=[pl.BlockSpec((MOBA_BLOCK, w), lambda b, j: (b * nblk + j, 0)),
                  pl.BlockSpec((seq, w), lambda b, j: (b, 0)),
                  pl.BlockSpec((seq, w), lambda b, j: (b, 0)),
                  whole(key_aux), whole(val_aux), whole(crow), whole(lams), whole(g)],
        out_specs=pl.BlockSpec((MOBA_BLOCK, w), lambda b, j: (b * nblk + j, 0)),
        out_shape=jax.ShapeDtypeStruct((batch * seq, w), BF16),
        scratch_shapes=_attend_scratch(2 * DIFF_HEADS, nblk),
        compiler_params=_params("arbitrary", "arbitrary"),
        name="diff",
    )(qd, kd, vd, key_aux, val_aux, crow, lams, g)


def _outproj_kernel(a_ref, d_ref, x_ref, wo_ref, g_ref, wq_ref, h_ref, q_ref, *, q_scale):
    wa = a_ref.shape[1]
    h = (x_ref[...]
         + jnp.dot(a_ref[...], wo_ref[:wa, :], preferred_element_type=F32)
         + jnp.dot(d_ref[...], wo_ref[wa:, :], preferred_element_type=F32))
    h_ref[...] = h
    qn = _rms(h, g_ref[...]).astype(BF16)
    q_ref[...] = (jnp.dot(qn, wq_ref[...], preferred_element_type=F32) * q_scale).astype(BF16)


def _outproj(a, d, x2, w_out, g, w_q, q_scale):
    t, dm = x2.shape
    tile = lambda w: pl.BlockSpec((TM_PROJ, w), lambda i: (i, 0))
    whole = lambda arr: pl.BlockSpec(arr.shape, lambda i: (0, 0))
    return pl.pallas_call(
        functools.partial(_outproj_kernel, q_scale=q_scale),
        grid=(t // TM_PROJ,),
        in_specs=[tile(a.shape[1]), tile(d.shape[1]), tile(dm), whole(w_out), whole(g), whole(w_q)],
        out_specs=[tile(dm), tile(dm)],
        out_shape=[jax.ShapeDtypeStruct((t, dm), F32), jax.ShapeDtypeStruct((t, dm), BF16)],
        compiler_params=_params("arbitrary"),
        name="outproj",
    )(a, d, x2, w_out, g, w_q)


def _memkv_kernel(m_ref, g_ref, w_ref, kv_ref):
    mn = _rms(m_ref[...], g_ref[...]).astype(BF16)
    kv_ref[...] = jnp.dot(mn, w_ref[...], preferred_element_type=F32).astype(BF16)


def _memkv(mem2, g, w_kv, mem_len):
    rows, dm = mem2.shape
    return pl.pallas_call(
        _memkv_kernel,
        grid=(rows // mem_len,),
        in_specs=[pl.BlockSpec((mem_len, dm), lambda i: (i, 0)),
                  pl.BlockSpec((1, dm), lambda i: (0, 0)),
                  pl.BlockSpec(w_kv.shape, lambda i: (0, 0))],
        out_specs=pl.BlockSpec((mem_len, w_kv.shape[1]), lambda i: (i, 0)),
        out_shape=jax.ShapeDtypeStruct((rows, w_kv.shape[1]), BF16),
        compiler_params=_params("arbitrary"),
        name="memkv",
    )(mem2, g, w_kv)


def _cross_kernel(q_ref, kv_ref, h_ref, wo_ref, g_ref, wr_ref, br_ref,
                  h2_ref, xn_ref, route_ref, counts_ref, run_ref):
    first = (pl.program_id(0) == 0) & (pl.program_id(1) == 0)

    @pl.when(first)
    def _():
        run_ref[...] = jnp.zeros_like(run_ref)

    tq, dm = h_ref.shape
    dh = dm // MEM_HEADS
    heads = []
    for h in range(MEM_HEADS):
        s = _dot_nt(q_ref[:, h * dh:(h + 1) * dh], kv_ref[:, h * dh:(h + 1) * dh])
        m = jnp.max(s, axis=-1, keepdims=True)
        e = jnp.exp(s - m)
        l = jnp.sum(e, axis=-1, keepdims=True)
        o = jnp.dot(e.astype(BF16), kv_ref[:, dm + h * dh:dm + (h + 1) * dh], preferred_element_type=F32)
        heads.append((o / l).astype(BF16))
    o = jnp.concatenate(heads, axis=1)
    h2 = h_ref[...] + jnp.dot(o, wo_ref[...], preferred_element_type=F32)
    h2_ref[...] = h2
    xn = _rms(h2, g_ref[...])
    xn_ref[...] = xn

    logits = jnp.dot(xn, wr_ref[...], precision=lax.Precision.HIGHEST, preferred_element_type=F32) + br_ref[...]
    lane = lax.broadcasted_iota(jnp.int32, logits.shape, 1)
    big = jnp.int32(ROUTE_COLS)

    def top1(vals):
        v = jnp.max(vals, axis=-1, keepdims=True)
        i = jnp.min(jnp.where(vals == v, lane, big), axis=-1, keepdims=True)
        return v, i

    gl = jnp.where(lane < N_GROUPS, logits, -jnp.inf)
    g_max, g_idx = top1(gl)
    g_w = 1.0 / jnp.sum(jnp.exp(gl - g_max), axis=-1, keepdims=True)
    lo = EXPERT_COL0 + g_idx * EXPERTS_PER_GROUP
    el = jnp.where((lane >= lo) & (lane < lo + EXPERTS_PER_GROUP), logits, -jnp.inf)
    v1, i1 = top1(el)
    v2, i2 = top1(jnp.where(lane == i1, -jnp.inf, el))
    e2 = jnp.exp(v2 - v1)
    w1 = g_w * (1.0 / (1.0 + e2))
    w2 = g_w * (e2 / (1.0 + e2))

    onehot = jnp.where((lane == i1) | (lane == i2), 1.0, 0.0)
    ri = lax.broadcasted_iota(jnp.int32, (tq, tq), 0)
    ci = lax.broadcasted_iota(jnp.int32, (tq, tq), 1)
    earlier = jnp.where(ci < ri, 1.0, 0.0).astype(BF16)
    before = jnp.dot(earlier, onehot.astype(BF16), preferred_element_type=F32) + run_ref[...]
    r1 = jnp.sum(jnp.where(lane == i1, before, 0.0), axis=-1, keepdims=True)
    r2 = jnp.sum(jnp.where(lane == i2, before, 0.0), axis=-1, keepdims=True)
    run_ref[...] = run_ref[...] + jnp.sum(onehot, axis=0, keepdims=True)
    counts_ref[...] = run_ref[...]

    rec = jnp.zeros(logits.shape, F32)
    for col, val in enumerate(((i1 - EXPERT_COL0).astype(F32), (i2 - EXPERT_COL0).astype(F32), w1, w2, r1, r2)):
        rec = jnp.where(lane == col, val, rec)
    route_ref[...] = rec


def _cross(qc, kv, h1, w_o, g, w_r, b_r, batch, seq, mem_len):
    t, dm = h1.shape
    nt = seq // TQ_CROSS
    tile = lambda w: pl.BlockSpec((TQ_CROSS, w), lambda b, i: (b * nt + i, 0))
    whole = lambda arr: pl.BlockSpec(arr.shape, lambda b, i: (0, 0))
    return pl.pallas_call(
        _cross_kernel,
        grid=(batch, nt),
        in_specs=[tile(dm), pl.BlockSpec((mem_len, kv.shape[1]), lambda b, i: (b, 0)), tile(dm),
                  whole(w_o), whole(g), whole(w_r), whole(b_r)],
        out_specs=[tile(dm), tile(dm), tile(ROUTE_COLS), pl.BlockSpec((1, ROUTE_COLS), lambda b, i: (0, 0))],
        out_shape=[jax.ShapeDtypeStruct((t, dm), F32), jax.ShapeDtypeStruct((t, dm), F32),
                   jax.ShapeDtypeStruct((t, ROUTE_COLS), F32), jax.ShapeDtypeStruct((1, ROUTE_COLS), F32)],
        scratch_shapes=[pltpu.VMEM((1, ROUTE_COLS), F32)],
        compiler_params=_params("arbitrary", "arbitrary"),
        name="cross",
    )(qc, kv, h1, w_o, g, w_r, b_r)


def _row_copy(src_ref, src_row, dst_ref, dst_row, sem):
    return pltpu.make_async_copy(src_ref.at[pl.ds(src_row, 1)], dst_ref.at[pl.ds(dst_row, 1)], sem)


def _dispatch_kernel(pos_ref, xn_ref, xs_in_ref, xs_ref, sem):
    del xs_in_ref
    i = pl.program_id(0)
    tq = xn_ref.shape[0]
    base = i * (2 * tq)

    def issue(t, c):
        _row_copy(xn_ref, t, xs_ref, pos_ref[base + 2 * t], sem).start()
        _row_copy(xn_ref, t, xs_ref, pos_ref[base + 2 * t + 1], sem).start()
        return c

    lax.fori_loop(0, tq, issue, 0)
    for _ in range(2):
        pltpu.make_async_copy(xn_ref, xs_ref.at[pl.ds(0, tq)], sem).wait()


def _dispatch(pos, xn, xs_zero):
    t, dm = xn.shape
    return pl.pallas_call(
        _dispatch_kernel,
        grid_spec=pltpu.PrefetchScalarGridSpec(
            num_scalar_prefetch=1,
            grid=(t // TQ_ROWS,),
            in_specs=[pl.BlockSpec((TQ_ROWS, dm), lambda i, pos: (i, 0)),
                      pl.BlockSpec(memory_space=pl.ANY)],
            out_specs=pl.BlockSpec(memory_space=pl.ANY),
            scratch_shapes=[pltpu.SemaphoreType.DMA]),
        out_shape=jax.ShapeDtypeStruct(xs_zero.shape, xs_zero.dtype),
        input_output_aliases={2: 0},
        compiler_params=_params("arbitrary"),
        name="dispatch",
    )(pos, xn, xs_zero)


def _experts_kernel(te_ref, nused_ref, xs_ref, wgu_ref, wd_ref, y_ref):
    i = pl.program_id(0)
    ff = wd_ref.shape[0]

    @pl.when(i < nused_ref[0])
    def _():
        gu = jnp.dot(xs_ref[...].astype(BF16), wgu_ref[...], preferred_element_type=F32)
        g = gu[:, :ff]
        hh = (g * jax.nn.sigmoid(g)) * gu[:, ff:]
        y_ref[...] = jnp.dot(hh.astype(BF16), wd_ref[...], preferred_element_type=F32)

    @pl.when(i >= nused_ref[0])
    def _():
        y_ref[...] = jnp.zeros_like(y_ref)


def _experts(tile_expert, n_used, xs, wgu, wd):
    rows, dm = xs.shape
    ff = wd.shape[1]
    return pl.pallas_call(
        _experts_kernel,
        grid_spec=pltpu.PrefetchScalarGridSpec(
            num_scalar_prefetch=2,
            grid=(rows // TM_EXPERT,),
            in_specs=[pl.BlockSpec((TM_EXPERT, dm), lambda i, te, nu: (i, 0)),
                      pl.BlockSpec((None, dm, 2 * ff), lambda i, te, nu: (te[i], 0, 0)),
                      pl.BlockSpec((None, ff, dm), lambda i, te, nu: (te[i], 0, 0))],
            out_specs=pl.BlockSpec((TM_EXPERT, dm), lambda i, te, nu: (i, 0))),
        out_shape=jax.ShapeDtypeStruct((rows, dm), F32),
        compiler_params=_params("arbitrary"),
        name="experts",
    )(tile_expert, n_used, xs, wgu, wd)


def _combine_kernel(pos_ref, h2_ref, route_ref, g_ref, y_ref, o_ref, ybuf, sem):
    i = pl.program_id(0)
    tq = h2_ref.shape[0]
    base = i * (2 * tq)

    def issue(t, c):
        _row_copy(y_ref, pos_ref[base + 2 * t], ybuf, t, sem).start()
        _row_copy(y_ref, pos_ref[base + 2 * t + 1], ybuf, tq + t, sem).start()
        return c

    lax.fori_loop(0, tq, issue, 0)
    pltpu.make_async_copy(y_ref.at[pl.ds(0, 2 * tq)], ybuf, sem).wait()

    rec = route_ref[...]
    h = h2_ref[...] + rec[:, 2:3] * ybuf[:tq, :] + rec[:, 3:4] * ybuf[tq:, :]
    o_ref[...] = _rms(h, g_ref[...])


def _combine(pos, h2, route, g, y):
    t, dm = h2.shape
    return pl.pallas_call(
        _combine_kernel,
        grid_spec=pltpu.PrefetchScalarGridSpec(
            num_scalar_prefetch=1,
            grid=(t // TQ_ROWS,),
            in_specs=[pl.BlockSpec((TQ_ROWS, dm), lambda i, pos: (i, 0)),
                      pl.BlockSpec((TQ_ROWS, ROUTE_COLS), lambda i, pos: (i, 0)),
                      pl.BlockSpec((1, dm), lambda i, pos: (0, 0)),
                      pl.BlockSpec(memory_space=pl.ANY)],
            out_specs=pl.BlockSpec((TQ_ROWS, dm), lambda i, pos: (i, 0)),
            scratch_shapes=[pltpu.VMEM((2 * TQ_ROWS, dm), F32), pltpu.SemaphoreType.DMA]),
        out_shape=jax.ShapeDtypeStruct((t, dm), F32),
        compiler_params=_params("arbitrary"),
        name="combine",
    )(pos, h2, route, g, y)


def kernel(x, mem, norm_mix, w_in, lambda_q1, lambda_k1, lambda_q2, lambda_k2, diff_subln, norm_moba_out, w_out, norm_mem_q, norm_mem_kv, w_mem_q, w_mem_kv, w_mem_o, norm_ffn, w_router_group, b_router_group, w_router_expert, b_router_expert, w_expert_gate, w_expert_up, w_expert_down, norm_final):
    batch, seq, dm = x.shape
    mem_len = mem.shape[1]
    depth = w_in.shape[0]
    t = batch * seq
    assert seq % MOBA_BLOCK == 0 and seq % TQ_CROSS == 0 and t % TM_PROJ == 0

    h = x.reshape(t, dm)
    mem2 = mem.reshape(batch * mem_len, dm)
    row = lambda v: v.reshape(1, -1).astype(F32)
    for l in range(depth):
        lam_init = 0.8 - 0.6 * math.exp(-0.3 * l)
        q_scale = HEAD_DIM ** -0.5 * LOG2E
        qa, ka, va, qd, kd, vd = _inproj(h, row(norm_mix[l]), w_in[l].astype(BF16),
                                         (q_scale, 1.0, 1.0, q_scale, 1.0, 1.0))
        key_aux, val_aux = _key_aux(seq), _value_aux()
        a = _moba(qa, ka, va, key_aux, val_aux, row(norm_moba_out[l]), batch, seq)
        lams = jnp.stack([lambda_q1[l], lambda_k1[l], lambda_q2[l], lambda_k2[l]]).astype(F32)
        d = _diff(qd, kd, vd, key_aux, val_aux, lams, row(diff_subln[l]), batch, seq, lam_init)
        h1, qc = _outproj(a, d, h, w_out[l].astype(BF16), row(norm_mem_q[l]), w_mem_q[l].astype(BF16),
                          (dm // MEM_HEADS) ** -0.5)
        kv = _memkv(mem2, row(norm_mem_kv[l]), w_mem_kv[l].astype(BF16), mem_len)

        pad = ROUTE_COLS - N_GROUPS - N_EXPERTS
        w_r = jnp.pad(jnp.concatenate([w_router_group[l], w_router_expert[l]], axis=1).astype(F32), ((0, 0), (0, pad)))
        b_r = jnp.pad(jnp.concatenate([b_router_group[l], b_router_expert[l]]).astype(F32), (0, pad)).reshape(1, -1)
        h2, xn, route, counts = _cross(qc, kv, h1, w_mem_o[l].astype(BF16), row(norm_ffn[l]), w_r, b_r,
                                       batch, seq, mem_len)

        tm = TM_EXPERT
        n_tiles = (2 * t) // tm + N_EXPERTS
        cnt = counts[0, EXPERT_COL0:EXPERT_COL0 + N_EXPERTS].astype(jnp.int32)
        padded = ((cnt + tm - 1) // tm) * tm
        ends = jnp.cumsum(padded)
        offs = ends - padded
        eidx = route[:, 0:2].astype(jnp.int32)
        pos = (offs[eidx] + route[:, 4:6].astype(jnp.int32)).reshape(2 * t)
        tile_start = jnp.arange(n_tiles, dtype=jnp.int32) * tm
        tile_expert = jnp.minimum(jnp.sum((ends[None, :] <= tile_start[:, None]).astype(jnp.int32), axis=1),
                                  N_EXPERTS - 1)
        n_used = (ends[-1:] // tm).astype(jnp.int32)

        xs = _dispatch(pos, xn, jnp.zeros((n_tiles * tm, dm), F32))
        wgu = jnp.concatenate([w_expert_gate[l], w_expert_up[l]], axis=2).astype(BF16)
        y = _experts(tile_expert, n_used, xs, wgu, w_expert_down[l].astype(BF16))
        assert depth == 1
        h = _combine(pos, h2, route, row(norm_final), y)
    return h.reshape(batch, seq, dm)
```

```python
import functools
import math

import jax
import jax.numpy as jnp
import numpy as np
from jax import lax
from jax.experimental import pallas as pl
from jax.experimental.pallas import tpu as pltpu

F32 = jnp.float32
BF16 = jnp.bfloat16

HEAD_DIM = 64
MOBA_HEADS = 8
MOBA_WIDTH = MOBA_HEADS * HEAD_DIM
MOBA_BLOCK = 256
MOBA_TOPK = 3
DIFF_HEADS = 4
DIFF_V_DIM = 2 * HEAD_DIM
DIFF_WIDTH = DIFF_HEADS * DIFF_V_DIM
MEM_HEADS = 4
N_GROUPS = 4
EXPERTS_PER_GROUP = 8
N_EXPERTS = N_GROUPS * EXPERTS_PER_GROUP
RMS_EPS = 1e-6
NEG_INF = -1e30
LOG2E = math.log2(math.e)
LANES = 128
ROUTE_COLS = LANES
EXPERT_COL0 = N_GROUPS

TM_PROJ = 512
TQ_ATTN = MOBA_BLOCK
TQ_CROSS = 512
TM_EXPERT = 256
VMEM_LIMIT = 56 * 1024 * 1024


def _alibi_slopes(n):
    return [2.0 ** (-8.0 * (i + 1) / n) for i in range(n)]


def _rms(x, g):
    y = x * lax.rsqrt(jnp.mean(x * x, axis=-1, keepdims=True) + RMS_EPS)
    return y * g


def _dot_nt(a, b):
    return lax.dot_general(a, b, (((1,), (1,)), ((), ())), preferred_element_type=F32)


def _params(*sem):
    return pltpu.CompilerParams(dimension_semantics=sem, vmem_limit_bytes=VMEM_LIMIT)


def _inproj_kernel(x_ref, g_ref, w_ref, *out_refs, scales):
    xn = _rms(x_ref[...], g_ref[...]).astype(BF16)
    width = out_refs[0].shape[1]
    for i, o_ref in enumerate(out_refs):
        y = jnp.dot(xn, w_ref[:, i * width:(i + 1) * width], preferred_element_type=F32)
        o_ref[...] = (y if scales[i] == 1.0 else y * scales[i]).astype(BF16)


def _inproj(x2, g, w, scales):
    t, d = x2.shape
    n_out = w.shape[1] // MOBA_WIDTH
    return pl.pallas_call(
        functools.partial(_inproj_kernel, scales=scales),
        grid=(t // TM_PROJ,),
        in_specs=[pl.BlockSpec((TM_PROJ, d), lambda i: (i, 0)),
                  pl.BlockSpec((1, d), lambda i: (0, 0)),
                  pl.BlockSpec(w.shape, lambda i: (0, 0))],
        out_specs=[pl.BlockSpec((TM_PROJ, MOBA_WIDTH), lambda i: (i, 0))] * n_out,
        out_shape=[jax.ShapeDtypeStruct((t, MOBA_WIDTH), BF16)] * n_out,
        compiler_params=_params("arbitrary"),
        name="inproj",
    )(x2, g, w)


AUX_PEN0 = 0
AUX_BLK0 = 64
AUX_OFF0 = 68
AUX_ROW = 72
N_SPLIT = 4


def _bf16_pieces(c):
    pieces, rem = [], np.float64(c)
    for _ in range(N_SPLIT):
        p = np.float64(np.asarray(rem).astype(BF16))
        pieces.append(p)
        rem = rem - p
    return pieces


def _key_aux(seq):
    pos = np.arange(seq)
    blk, off = pos // MOBA_BLOCK, pos % MOBA_BLOCK
    assert blk.max() < 8, "penalty lanes hold 8 blocks per head"
    a = np.zeros((seq, LANES), np.float32)
    for h in range(MOBA_HEADS):
        a[pos, AUX_PEN0 + h * 8 + blk] = 1.0
    a[:, AUX_BLK0:AUX_BLK0 + N_SPLIT] = (blk * MOBA_BLOCK)[:, None]
    a[:, AUX_OFF0:AUX_OFF0 + N_SPLIT] = off[:, None]
    a[:, AUX_ROW] = 1.0
    return jnp.asarray(a, BF16)


def _value_aux():
    a = np.zeros((MOBA_BLOCK, LANES), np.float32)
    a[:, 0] = 1.0
    return jnp.asarray(a, BF16)


def _query_aux_rows(slopes):
    a = np.zeros((len(slopes), LANES), np.float32)
    for i, s in enumerate(slopes):
        a[i, AUX_BLK0:AUX_BLK0 + N_SPLIT] = _bf16_pieces(s * LOG2E)
        a[i, AUX_OFF0:AUX_OFF0 + N_SPLIT] = _bf16_pieces(s * LOG2E)
    return jnp.asarray(a)


def _query_aux(crow, slope, j, pen_t=None, pen_lo=0):
    blk = MOBA_BLOCK
    lane = lax.broadcasted_iota(jnp.int32, (blk, LANES), 1)
    t_q = (j * blk).astype(F32) + lax.broadcasted_iota(jnp.int32, (blk, LANES), 0).astype(F32)
    aux = jnp.where(lane == AUX_ROW, (-slope * LOG2E) * t_q, crow)
    if pen_t is not None:
        aux = jnp.where((lane >= pen_lo) & (lane < pen_lo + 8), pen_t, aux)
    return aux.astype(BF16)


def _attend(qaug_ref, k_ref, v_ref, kaux_ref, vaux_ref, s_ref, sown_ref, mrun_ref, acc_ref, j, kcols, vcols):
    n_items = qaug_ref.shape[0]
    blk = MOBA_BLOCK
    causal = (lax.broadcasted_iota(jnp.int32, (blk, blk), 0) >= lax.broadcasted_iota(jnp.int32, (blk, blk), 1))
    own = pl.ds(pl.multiple_of(j * blk, blk), blk)
    block_rows = lambda n: pl.ds(pl.multiple_of(n * blk, blk), blk)

    def scores(i, rows, kaux):
        return _dot_nt(qaug_ref[i], jnp.concatenate([k_ref[rows, kcols[i]], kaux], axis=1))

    def half_max(s):
        return jnp.maximum(s[:, :LANES], s[:, LANES:])

    kaux_own = kaux_ref[own, :]
    for i in range(n_items):
        s = jnp.where(causal, scores(i, own, kaux_own), NEG_INF)
        sown_ref[i] = s
        mrun_ref[i] = half_max(s)

    def pass_a(n, c):
        kaux_n = kaux_ref[block_rows(n), :]
        for i in range(n_items):
            s = scores(i, block_rows(n), kaux_n)
            s_ref[i, n] = s
            mrun_ref[i] = jnp.maximum(mrun_ref[i], half_max(s))
        return c

    lax.fori_loop(0, j, pass_a, 0)

    for i in range(n_items):
        mrun_ref[i] = jnp.broadcast_to(jnp.max(mrun_ref[i], axis=1, keepdims=True), (blk, LANES))

    vaux = vaux_ref[...]

    def pv(i, s, rows):
        m = mrun_ref[i]
        p = jnp.concatenate([jnp.exp2(s[:, :LANES] - m), jnp.exp2(s[:, LANES:] - m)], axis=1).astype(BF16)
        return jnp.dot(p, jnp.concatenate([v_ref[rows, vcols[i]], vaux], axis=1), preferred_element_type=F32)

    for i in range(n_items):
        acc_ref[i] = pv(i, sown_ref[i], own)

    def pass_b(n, c):
        for i in range(n_items):
            acc_ref[i] += pv(i, s_ref[i, n], block_rows(n))
        return c

    lax.fori_loop(0, j, pass_b, 0)

    outs = []
    for i in range(n_items):
        a = acc_ref[i]
        outs.append(a[:, :LANES] * (1.0 / a[:, LANES:LANES + 1]))
    return outs


def _attend_scratch(n_items, nblk):
    blk = MOBA_BLOCK
    return [pltpu.VMEM((n_items, blk, 2 * LANES), BF16),
            pltpu.VMEM((n_items, nblk, blk, blk), F32),
            pltpu.VMEM((n_items, blk, blk), F32),
            pltpu.VMEM((n_items, blk, LANES), F32),
            pltpu.VMEM((n_items, blk, 2 * LANES), F32)]


def _moba_kernel(q_ref, k_ref, v_ref, kaux_ref, vaux_ref, crow_ref, g_ref, o_ref,
                 kmean_ref, kbd_ref, qaug_ref, s_ref, sown_ref, mrun_ref, acc_ref):
    j = pl.program_id(1)
    blk = MOBA_BLOCK
    nblk = s_ref.shape[1]
    slopes = _alibi_slopes(MOBA_HEADS)
    n_slabs = MOBA_WIDTH // LANES

    @pl.when(j == 0)
    def _():
        kmean_ref[...] = jnp.zeros_like(kmean_ref)
        for n in range(nblk):
            kb = k_ref[n * blk:(n + 1) * blk, :].astype(F32)
            kmean_ref[n:n + 1, :] = jnp.sum(kb, axis=0, keepdims=True) * (1.0 / blk)
        col_head = lax.broadcasted_iota(jnp.int32, kmean_ref.shape, 1) // HEAD_DIM
        for h in range(MOBA_HEADS):
            kbd_ref[h * 8:(h + 1) * 8, :] = jnp.where(col_head == h, kmean_ref[...], 0.0)

    gate = lax.dot_general(kbd_ref[...], q_ref[...].astype(F32), (((1,), (1,)), ((), ())),
                           precision=lax.Precision.HIGHEST, preferred_element_type=F32)
    n_iota = lax.broadcasted_iota(jnp.int32, (8, blk), 0)
    pens = []
    for h in range(MOBA_HEADS):
        g = gate[h * 8:(h + 1) * 8, :]
        rank = jnp.zeros((8, blk), jnp.int32)
        for m in range(nblk):
            gm = g[m:m + 1, :]
            ahead = (gm > g) | ((gm == g) & (m < n_iota))
            rank = rank + jnp.where(ahead, 1, 0) * (m < j).astype(jnp.int32)
        pens.append(jnp.where((n_iota < j) & (rank >= MOBA_TOPK), NEG_INF, 0.0))
    pen_t = jnp.concatenate(pens + [jnp.zeros((LANES - 8 * MOBA_HEADS, blk), F32)], axis=0).T

    lane = lax.broadcasted_iota(jnp.int32, (blk, LANES), 1)
    for h in range(MOBA_HEADS):
        half = h % 2
        q_slab = q_ref[:, (h // 2) * LANES:(h // 2 + 1) * LANES]
        in_head = (lane >= half * HEAD_DIM) & (lane < (half + 1) * HEAD_DIM)
        aux = _query_aux(crow_ref[h:h + 1, :], slopes[h], j, pen_t, AUX_PEN0 + h * 8)
        qaug_ref[h] = jnp.concatenate([jnp.where(in_head, q_slab, 0).astype(BF16), aux], axis=1)

    cols = [slice((h // 2) * LANES, (h // 2 + 1) * LANES) for h in range(MOBA_HEADS)]
    outs = _attend(qaug_ref, k_ref, v_ref, kaux_ref, vaux_ref, s_ref, sown_ref, mrun_ref, acc_ref, j, cols, cols)
    o = jnp.concatenate([jnp.where(lane < HEAD_DIM, outs[2 * p], outs[2 * p + 1]) for p in range(n_slabs)], axis=1)
    o_ref[...] = _rms(o, g_ref[...]).astype(BF16)


def _moba(qa, ka, va, key_aux, val_aux, g, batch, seq):
    nblk = seq // MOBA_BLOCK
    w = MOBA_WIDTH
    crow = _query_aux_rows(_alibi_slopes(MOBA_HEADS))
    whole = lambda arr: pl.BlockSpec(arr.shape, lambda b, j: (0, 0))
    return pl.pallas_call(
        _moba_kernel,
        grid=(batch, nblk),
        in_specs=[pl.BlockSpec((MOBA_BLOCK, w), lambda b, j: (b * nblk + j, 0)),
                  pl.BlockSpec((seq, w), lambda b, j: (b, 0)),
                  pl.BlockSpec((seq, w), lambda b, j: (b, 0)),
                  whole(key_aux), whole(val_aux), whole(crow), whole(g)],
        out_specs=pl.BlockSpec((MOBA_BLOCK, w), lambda b, j: (b * nblk + j, 0)),
        out_shape=jax.ShapeDtypeStruct((batch * seq, w), BF16),
        scratch_shapes=[pltpu.VMEM((8, w), F32), pltpu.VMEM((8 * MOBA_HEADS, w), F32)]
        + _attend_scratch(MOBA_HEADS, nblk),
        compiler_params=_params("arbitrary", "arbitrary"),
        name="moba",
    )(qa, ka, va, key_aux, val_aux, crow, g)


def _diff_kernel(q_ref, k_ref, v_ref, kaux_ref, vaux_ref, crow_ref, lam_ref, g_ref, o_ref,
                 qaug_ref, s_ref, sown_ref, mrun_ref, acc_ref, *, lam_init):
    j = pl.program_id(1)
    blk = MOBA_BLOCK
    slopes = _alibi_slopes(DIFF_HEADS)
    lane = lax.broadcasted_iota(jnp.int32, (blk, LANES), 1)

    lv = lam_ref[...]
    lam = (jnp.exp(jnp.sum(lv[0:1] * lv[1:2], axis=1, keepdims=True))
           - jnp.exp(jnp.sum(lv[2:3] * lv[3:4], axis=1, keepdims=True)) + lam_init)

    for i in range(2 * DIFF_HEADS):
        h, c = i // 2, i % 2
        q_slab = q_ref[:, h * LANES:(h + 1) * LANES]
        in_map = (lane >= c * HEAD_DIM) & (lane < (c + 1) * HEAD_DIM)
        aux = _query_aux(crow_ref[i:i + 1, :], slopes[h], j)
        qaug_ref[i] = jnp.concatenate([jnp.where(in_map, q_slab, 0).astype(BF16), aux], axis=1)

    cols = [slice((i // 2) * LANES, (i // 2 + 1) * LANES) for i in range(2 * DIFF_HEADS)]
    outs = _attend(qaug_ref, k_ref, v_ref, kaux_ref, vaux_ref, s_ref, sown_ref, mrun_ref, acc_ref, j, cols, cols)
    for h in range(DIFF_HEADS):
        o = outs[2 * h] - lam * outs[2 * h + 1]
        o_ref[:, h * LANES:(h + 1) * LANES] = (_rms(o, g_ref[...]) * (1.0 - lam_init)).astype(BF16)


def _diff(qd, kd, vd, key_aux, val_aux, lams, g, batch, seq, lam_init):
    nblk = seq // MOBA_BLOCK
    w = DIFF_WIDTH
    crow = _query_aux_rows([s for s in _alibi_slopes(DIFF_HEADS) for _ in range(2)])
    whole = lambda arr: pl.BlockSpec(arr.shape, lambda b, j: (0, 0))
    return pl.pallas_call(
        functools.partial(_diff_kernel, lam_init=lam_init),
        grid=(batch, nblk),
        in_specs=[pl.BlockSpec((MOBA_BLOCK, w), lambda b, j: (b * nblk + j, 0)),
                  pl.BlockSpec((seq, w), lambda b, j: (b, 0)),
                  pl.BlockSpec((seq, w), lambda b, j: (b, 0)),
                  whole(key_aux), whole(val_aux), whole(crow), whole(lams), whole(g)],
        out_specs=pl.BlockSpec((MOBA_BLOCK, w), lambda b, j: (b * nblk + j, 0)),
        out_shape=jax.ShapeDtypeStruct((batch * seq, w), BF16),
        scratch_shapes=_attend_scratch(2 * DIFF_HEADS, nblk),
        compiler_params=_params("arbitrary", "arbitrary"),
        name="diff",
    )(qd, kd, vd, key_aux, val_aux, crow, lams, g)


def _outproj_kernel(a_ref, d_ref, x_ref, wo_ref, g_ref, wq_ref, h_ref, q_ref, *, q_scale):
    wa = a_ref.shape[1]
    h = (x_ref[...]
         + jnp.dot(a_ref[...], wo_ref[:wa, :], preferred_element_type=F32)
         + jnp.dot(d_ref[...], wo_ref[wa:, :], preferred_element_type=F32))
    h_ref[...] = h
    qn = _rms(h, g_ref[...]).astype(BF16)
    q_ref[...] = (jnp.dot(qn, wq_ref[...], preferred_element_type=F32) * q_scale).astype(BF16)


def _outproj(a, d, x2, w_out, g, w_q, q_scale):
    t, dm = x2.shape
    tile = lambda w: pl.BlockSpec((TM_PROJ, w), lambda i: (i, 0))
    whole = lambda arr: pl.BlockSpec(arr.shape, lambda i: (0, 0))
    return pl.pallas_call(
        functools.partial(_outproj_kernel, q_scale=q_scale),
        grid=(t // TM_PROJ,),
        in_specs=[tile(a.shape[1]), tile(d.shape[1]), tile(dm), whole(w_out), whole(g), whole(w_q)],
        out_specs=[tile(dm), tile(dm)],
        out_shape=[jax.ShapeDtypeStruct((t, dm), F32), jax.ShapeDtypeStruct((t, dm), BF16)],
        compiler_params=_params("arbitrary"),
        name="outproj",
    )(a, d, x2, w_out, g, w_q)


def _memkv_kernel(m_ref, g_ref, w_ref, kv_ref):
    mn = _rms(m_ref[...], g_ref[...]).astype(BF16)
    kv_ref[...] = jnp.dot(mn, w_ref[...], preferred_element_type=F32).astype(BF16)


def _memkv(mem2, g, w_kv, mem_len):
    rows, dm = mem2.shape
    return pl.pallas_call(
        _memkv_kernel,
        grid=(rows // mem_len,),
        in_specs=[pl.BlockSpec((mem_len, dm), lambda i: (i, 0)),
                  pl.BlockSpec((1, dm), lambda i: (0, 0)),
                  pl.BlockSpec(w_kv.shape, lambda i: (0, 0))],
        out_specs=pl.BlockSpec((mem_len, w_kv.shape[1]), lambda i: (i, 0)),
        out_shape=jax.ShapeDtypeStruct((rows, w_kv.shape[1]), BF16),
        compiler_params=_params("arbitrary"),
        name="memkv",
    )(mem2, g, w_kv)


def _cross_kernel(q_ref, kv_ref, h_ref, wo_ref, g_ref, wr_ref, br_ref,
                  h2_ref, xn_ref, route_ref, counts_ref, run_ref):
    first = (pl.program_id(0) == 0) & (pl.program_id(1) == 0)

    @pl.when(first)
    def _():
        run_ref[...] = jnp.zeros_like(run_ref)

    tq, dm = h_ref.shape
    dh = dm // MEM_HEADS
    heads = []
    for h in range(MEM_HEADS):
        s = _dot_nt(q_ref[:, h * dh:(h + 1) * dh], kv_ref[:, h * dh:(h + 1) * dh])
        m = jnp.max(s, axis=-1, keepdims=True)
        e = jnp.exp(s - m)
        l = jnp.sum(e, axis=-1, keepdims=True)
        o = jnp.dot(e.astype(BF16), kv_ref[:, dm + h * dh:dm + (h + 1) * dh], preferred_element_type=F32)
        heads.append((o * (1.0 / l)).astype(BF16))
    o = jnp.concatenate(heads, axis=1)
    h2 = h_ref[...] + jnp.dot(o, wo_ref[...], preferred_element_type=F32)
    h2_ref[...] = h2
    xn = _rms(h2, g_ref[...])
    _to_token_major(xn_ref, xn)

    x_hi = xn.astype(BF16)
    x_lo = (xn - x_hi.astype(F32)).astype(BF16)
    hi = jnp.dot(x_hi, wr_ref[...], preferred_element_type=F32)
    lo_hi = jnp.dot(x_lo, wr_ref[:, :ROUTE_COLS], preferred_element_type=F32)
    logits = hi[:, :ROUTE_COLS] + (hi[:, ROUTE_COLS:] + lo_hi) + br_ref[...]
    lane = lax.broadcasted_iota(jnp.int32, logits.shape, 1).astype(F32)
    big = float(ROUTE_COLS)

    def top1(vals):
        v = jnp.max(vals, axis=-1, keepdims=True)
        i = jnp.min(jnp.where(vals == v, lane, big), axis=-1, keepdims=True)
        return v, i

    gl = jnp.where(lane < N_GROUPS, logits, -jnp.inf)
    g_max, g_idx = top1(gl)
    g_w = 1.0 / jnp.sum(jnp.exp(gl - g_max), axis=-1, keepdims=True)
    lo = EXPERT_COL0 + g_idx * EXPERTS_PER_GROUP
    el = jnp.where((lane >= lo) & (lane < lo + EXPERTS_PER_GROUP), logits, -jnp.inf)
    v1, i1 = top1(el)
    v2, i2 = top1(jnp.where(lane == i1, -jnp.inf, el))
    e2 = jnp.exp(v2 - v1)
    w1 = g_w * (1.0 / (1.0 + e2))
    w2 = g_w * (e2 / (1.0 + e2))

    onehot = jnp.where((lane == i1) | (lane == i2), 1.0, 0.0)
    ri = lax.broadcasted_iota(jnp.int32, (tq, tq), 0)
    ci = lax.broadcasted_iota(jnp.int32, (tq, tq), 1)
    earlier = jnp.where(ci < ri, 1.0, 0.0).astype(BF16)
    before = jnp.dot(earlier, onehot.astype(BF16), preferred_element_type=F32) + run_ref[...]
    r1 = jnp.sum(jnp.where(lane == i1, before, 0.0), axis=-1, keepdims=True)
    r2 = jnp.sum(jnp.where(lane == i2, before, 0.0), axis=-1, keepdims=True)
    run_ref[...] = run_ref[...] + jnp.sum(onehot, axis=0, keepdims=True)
    counts_ref[...] = run_ref[...]

    rec = jnp.zeros(logits.shape, F32)
    for col, val in enumerate((i1 - EXPERT_COL0, i2 - EXPERT_COL0, w1, w2, r1, r2)):
        rec = jnp.where(lane == col, val, rec)
    route_ref[...] = rec


def _cross(qc, kv, h1, w_o, g, w_r, b_r, batch, seq, mem_len):
    t, dm = h1.shape
    nt = seq // TQ_CROSS
    tile = lambda w: pl.BlockSpec((TQ_CROSS, w), lambda b, i: (b * nt + i, 0))
    whole = lambda arr: pl.BlockSpec(arr.shape, lambda b, i: (0, 0))
    return pl.pallas_call(
        _cross_kernel,
        grid=(batch, nt),
        in_specs=[tile(dm), pl.BlockSpec((mem_len, kv.shape[1]), lambda b, i: (b, 0)), tile(dm),
                  whole(w_o), whole(g), whole(w_r), whole(b_r)],
        out_specs=[tile(dm), pl.BlockSpec((TQ_CROSS * SUB, LANES), lambda b, i: (b * nt + i, 0)),
                   tile(ROUTE_COLS), pl.BlockSpec((1, ROUTE_COLS), lambda b, i: (0, 0))],
        out_shape=[jax.ShapeDtypeStruct((t, dm), F32), jax.ShapeDtypeStruct((t * SUB, LANES), F32),
                   jax.ShapeDtypeStruct((t, ROUTE_COLS), F32), jax.ShapeDtypeStruct((1, ROUTE_COLS), F32)],
        scratch_shapes=[pltpu.VMEM((1, ROUTE_COLS), F32)],
        compiler_params=_params("arbitrary", "arbitrary"),
        name="cross",
    )(qc, kv, h1, w_o, g, w_r, b_r)


SUB = 8


def _to_token_major(ref, x):
    rows = x.shape[0]
    for c in range(x.shape[1] // LANES):
        ref[pl.ds(c, rows, stride=SUB), :] = x[:, c * LANES:(c + 1) * LANES]


def _from_token_major(ref, rows):
    n_chunks = ref.shape[0] // rows
    return jnp.concatenate([ref[pl.ds(c, rows, stride=SUB), :] for c in range(n_chunks)], axis=1)


def _experts_kernel(te_ref, nused_ref, dst_ref, xn_ref, wgu_ref, wd_ref, yo_ref, xbuf, ybuf, gsem, ssem, *, n_tok):
    i = pl.program_id(0)
    last = pl.num_programs(0) - 1
    n_used = nused_ref[0]
    tm = xbuf.shape[1] // SUB
    ff = wd_ref.shape[0]
    slot = i % 2
    token_rows = lambda tok: pl.ds(pl.multiple_of(tok * SUB, SUB), SUB)

    def for_each_slot(tile, fn):
        for q in range(tm // LANES):
            for lane in range(LANES):
                fn(dst_ref[tile * (tm // LANES) + q, lane], q * LANES + lane)

    def gather(tile, buf):
        for_each_slot(tile, lambda dst, r: pltpu.make_async_copy(
            xn_ref.at[token_rows(dst & (n_tok - 1))], xbuf.at[buf, token_rows(r)], gsem.at[buf]).start())

    def scatter(tile, buf):
        for_each_slot(tile, lambda dst, r: pltpu.make_async_copy(
            ybuf.at[buf, token_rows(r)], yo_ref.at[token_rows(dst)], ssem.at[buf]).start())

    def wait_gather(buf):
        pltpu.make_async_copy(xn_ref.at[pl.ds(0, tm * SUB)], xbuf.at[buf], gsem.at[buf]).wait()

    def wait_scatter(buf):
        pltpu.make_async_copy(ybuf.at[buf], yo_ref.at[pl.ds(0, tm * SUB)], ssem.at[buf]).wait()

    @pl.when((i == 0) & (n_used > 0))
    def _():
        gather(0, 0)

    @pl.when((i >= 2) & (i - 2 < n_used))
    def _():
        wait_scatter(slot)

    @pl.when(i < n_used)
    def _():
        wait_gather(slot)
        gather(jnp.minimum(i + 1, n_used - 1), 1 - slot)
        x = _from_token_major(xbuf.at[slot], tm).astype(BF16)
        gu = jnp.dot(x, wgu_ref[...], preferred_element_type=F32)
        g = gu[:, :ff]
        hh = (g * jax.nn.sigmoid(g)) * gu[:, ff:]
        _to_token_major(ybuf.at[slot], jnp.dot(hh.astype(BF16), wd_ref[...], preferred_element_type=F32))
        scatter(i, slot)

        @pl.when(i + 1 == n_used)
        def _():
            wait_gather(1 - slot)

    @pl.when(i >= n_used)
    def _():
        ybuf[slot] = jnp.zeros(ybuf.shape[1:], F32)
        fill = pltpu.make_async_copy(
            ybuf.at[slot], yo_ref.at[pl.ds(pl.multiple_of(i * (tm * SUB), tm * SUB), tm * SUB)], ssem.at[slot])
        fill.start()
        fill.wait()

    @pl.when(i == last)
    def _():
        @pl.when((i >= 1) & (i - 1 < n_used))
        def _():
            wait_scatter(1 - slot)

        @pl.when(i < n_used)
        def _():
            wait_scatter(slot)


def _experts(tile_expert, n_used, dst, xn_tm, wgu, wd, tm):
    n_tok = xn_tm.shape[0] // SUB
    n_tiles = dst.shape[0] * LANES // tm
    _, dm, ff2 = wgu.shape
    assert n_tok & (n_tok - 1) == 0, "source token = dst & (n_tok - 1) needs a power-of-two token count"
    assert dm == SUB * LANES and tm % LANES == 0 and dst.shape[1] == LANES
    return pl.pallas_call(
        functools.partial(_experts_kernel, n_tok=n_tok),
        grid_spec=pltpu.PrefetchScalarGridSpec(
            num_scalar_prefetch=3,
            grid=(n_tiles,),
            in_specs=[pl.BlockSpec(memory_space=pl.ANY),
                      pl.BlockSpec((None, dm, ff2), lambda i, te, nu, ds: (te[i], 0, 0)),
                      pl.BlockSpec((None, ff2 // 2, dm), lambda i, te, nu, ds: (te[i], 0, 0))],
            out_specs=pl.BlockSpec(memory_space=pl.ANY),
            scratch_shapes=[pltpu.VMEM((2, tm * SUB, LANES), F32), pltpu.VMEM((2, tm * SUB, LANES), F32),
                            pltpu.SemaphoreType.DMA((2,)), pltpu.SemaphoreType.DMA((2,))]),
        out_shape=jax.ShapeDtypeStruct((n_tiles * tm * SUB, LANES), F32),
        compiler_params=_params("arbitrary"),
        name="experts",
    )(tile_expert, n_used, dst, xn_tm, wgu, wd)


def _combine_kernel(h2_ref, route_ref, y0_ref, y1_ref, g_ref, o_ref):
    rows = h2_ref.shape[0]
    rec = route_ref[...]
    h = (h2_ref[...] + rec[:, 2:3] * _from_token_major(y0_ref, rows)
         + rec[:, 3:4] * _from_token_major(y1_ref, rows))
    o_ref[...] = _rms(h, g_ref[...])


def _combine(h2, route, yo, g):
    t, dm = h2.shape
    nt = t // TM_PROJ
    tile = lambda w: pl.BlockSpec((TM_PROJ, w), lambda i: (i, 0))
    return pl.pallas_call(
        _combine_kernel,
        grid=(nt,),
        in_specs=[tile(dm), tile(ROUTE_COLS),
                  pl.BlockSpec((TM_PROJ * SUB, LANES), lambda i: (i, 0)),
                  pl.BlockSpec((TM_PROJ * SUB, LANES), lambda i: (i + nt, 0)),
                  pl.BlockSpec((1, dm), lambda i: (0, 0))],
        out_specs=tile(dm),
        out_shape=jax.ShapeDtypeStruct((t, dm), F32),
        compiler_params=_params("arbitrary"),
        name="combine",
    )(h2, route, yo, yo, g)


def kernel(x, mem, norm_mix, w_in, lambda_q1, lambda_k1, lambda_q2, lambda_k2, diff_subln, norm_moba_out, w_out, norm_mem_q, norm_mem_kv, w_mem_q, w_mem_kv, w_mem_o, norm_ffn, w_router_group, b_router_group, w_router_expert, b_router_expert, w_expert_gate, w_expert_up, w_expert_down, norm_final):
    batch, seq, dm = x.shape
    mem_len = mem.shape[1]
    depth = w_in.shape[0]
    t = batch * seq
    assert seq % MOBA_BLOCK == 0 and seq % TQ_CROSS == 0 and t % TM_PROJ == 0

    h = x.reshape(t, dm)
    mem2 = mem.reshape(batch * mem_len, dm)
    row = lambda v: v.reshape(1, -1).astype(F32)
    for l in range(depth):
        lam_init = 0.8 - 0.6 * math.exp(-0.3 * l)
        q_scale = HEAD_DIM ** -0.5 * LOG2E
        qa, ka, va, qd, kd, vd = _inproj(h, row(norm_mix[l]), w_in[l].astype(BF16),
                                         (q_scale, 1.0, 1.0, q_scale, 1.0, 1.0))
        key_aux, val_aux = _key_aux(seq), _value_aux()
        a = _moba(qa, ka, va, key_aux, val_aux, row(norm_moba_out[l]), batch, seq)
        lams = jnp.stack([lambda_q1[l], lambda_k1[l], lambda_q2[l], lambda_k2[l]]).astype(F32)
        d = _diff(qd, kd, vd, key_aux, val_aux, lams, row(diff_subln[l]), batch, seq, lam_init)
        h1, qc = _outproj(a, d, h, w_out[l].astype(BF16), row(norm_mem_q[l]), w_mem_q[l].astype(BF16),
                          (dm // MEM_HEADS) ** -0.5)
        kv = _memkv(mem2, row(norm_mem_kv[l]), w_mem_kv[l].astype(BF16), mem_len)

        pad = ROUTE_COLS - N_GROUPS - N_EXPERTS
        w_r = jnp.pad(jnp.concatenate([w_router_group[l], w_router_expert[l]], axis=1).astype(F32), ((0, 0), (0, pad)))
        w_r_hi = w_r.astype(BF16)
        w_r = jnp.concatenate([w_r_hi, (w_r - w_r_hi.astype(F32)).astype(BF16)], axis=1)
        b_r = jnp.pad(jnp.concatenate([b_router_group[l], b_router_expert[l]]).astype(F32), (0, pad)).reshape(1, -1)
        h2, xn, route, counts = _cross(qc, kv, h1, w_mem_o[l].astype(BF16), row(norm_ffn[l]), w_r, b_r,
                                       batch, seq, mem_len)

        tm = TM_EXPERT
        n_tiles = (2 * t) // tm + N_EXPERTS
        cnt = counts[0, EXPERT_COL0:EXPERT_COL0 + N_EXPERTS].astype(jnp.int32)
        padded = ((cnt + tm - 1) // tm) * tm
        ends = jnp.cumsum(padded)
        offs = ends - padded
        eidx = route[:, 0:2].astype(jnp.int32)
        pos = offs[eidx] + route[:, 4:6].astype(jnp.int32)
        tile_start = jnp.arange(n_tiles, dtype=jnp.int32) * tm
        tile_expert = jnp.minimum(jnp.sum((ends[None, :] <= tile_start[:, None]).astype(jnp.int32), axis=1),
                                  N_EXPERTS - 1)
        n_used = (ends[-1:] // tm).astype(jnp.int32)
        slot = jnp.arange(n_tiles * tm, dtype=jnp.int32)
        slot_expert = jnp.repeat(tile_expert, tm)
        pad_rank = slot - offs[slot_expert] - cnt[slot_expert]
        pads_before = jnp.cumsum(padded - cnt) - (padded - cnt)
        dummy = 2 * t + pads_before[slot_expert] + jnp.clip(pad_rank, 0, tm - 1)
        real_row = jnp.arange(t, dtype=jnp.int32)[:, None] + jnp.array([0, t], jnp.int32)[None, :]
        dst = dummy.at[pos.reshape(-1)].set(real_row.reshape(-1)).reshape(-1, LANES)

        wgu = jnp.concatenate([w_expert_gate[l], w_expert_up[l]], axis=2).astype(BF16)
        yo = _experts(tile_expert, n_used, dst, xn, wgu, w_expert_down[l].astype(BF16), tm)
        assert depth == 1
        h = _combine(h2, route, yo, row(norm_final))
    return h.reshape(batch, seq, dm)
```

```python
import functools
import math

import jax
import jax.numpy as jnp
import numpy as np
from jax import lax
from jax.experimental import pallas as pl
from jax.experimental.pallas import tpu as pltpu

F32 = jnp.float32
BF16 = jnp.bfloat16

HEAD_DIM = 64
MOBA_HEADS = 8
MOBA_WIDTH = MOBA_HEADS * HEAD_DIM
MOBA_BLOCK = 256
MOBA_TOPK = 3
DIFF_HEADS = 4
DIFF_V_DIM = 2 * HEAD_DIM
DIFF_WIDTH = DIFF_HEADS * DIFF_V_DIM
MEM_HEADS = 4
N_GROUPS = 4
EXPERTS_PER_GROUP = 8
N_EXPERTS = N_GROUPS * EXPERTS_PER_GROUP
RMS_EPS = 1e-6
NEG_INF = -1e30
LOG2E = math.log2(math.e)
LANES = 128
ROUTE_COLS = LANES
EXPERT_COL0 = N_GROUPS

TM_PROJ = 512
TQ_ATTN = MOBA_BLOCK
TQ_CROSS = 512
TM_EXPERT = 256
VMEM_LIMIT = 56 * 1024 * 1024


def _alibi_slopes(n):
    return [2.0 ** (-8.0 * (i + 1) / n) for i in range(n)]


def _rms(x, g):
    y = x * lax.rsqrt(jnp.mean(x * x, axis=-1, keepdims=True) + RMS_EPS)
    return y * g


def _dot_nt(a, b):
    return lax.dot_general(a, b, (((1,), (1,)), ((), ())), preferred_element_type=F32)


def _params(*sem):
    return pltpu.CompilerParams(dimension_semantics=sem, vmem_limit_bytes=VMEM_LIMIT)


def _inproj_kernel(x_ref, g_ref, w_ref, *out_refs, scales):
    xn = _rms(x_ref[...], g_ref[...]).astype(BF16)
    width = out_refs[0].shape[1]
    for i, o_ref in enumerate(out_refs):
        y = jnp.dot(xn, w_ref[:, i * width:(i + 1) * width], preferred_element_type=F32)
        o_ref[...] = (y if scales[i] == 1.0 else y * scales[i]).astype(BF16)


def _inproj(x2, g, w, scales):
    t, d = x2.shape
    n_out = w.shape[1] // MOBA_WIDTH
    return pl.pallas_call(
        functools.partial(_inproj_kernel, scales=scales),
        grid=(t // TM_PROJ,),
        in_specs=[pl.BlockSpec((TM_PROJ, d), lambda i: (i, 0)),
                  pl.BlockSpec((1, d), lambda i: (0, 0)),
                  pl.BlockSpec(w.shape, lambda i: (0, 0))],
        out_specs=[pl.BlockSpec((TM_PROJ, MOBA_WIDTH), lambda i: (i, 0))] * n_out,
        out_shape=[jax.ShapeDtypeStruct((t, MOBA_WIDTH), BF16)] * n_out,
        compiler_params=_params("arbitrary"),
        name="inproj",
    )(x2, g, w)


AUX_PEN0 = 0
AUX_BLK0 = 64
AUX_OFF0 = 68
AUX_ROW = 72
N_SPLIT = 4


def _bf16_pieces(c):
    pieces, rem = [], np.float64(c)
    for _ in range(N_SPLIT):
        p = np.float64(np.asarray(rem).astype(BF16))
        pieces.append(p)
        rem = rem - p
    return pieces


def _key_aux(seq):
    pos = np.arange(seq)
    blk, off = pos // MOBA_BLOCK, pos % MOBA_BLOCK
    assert blk.max() < 8, "penalty lanes hold 8 blocks per head"
    a = np.zeros((seq, LANES), np.float32)
    for h in range(MOBA_HEADS):
        a[pos, AUX_PEN0 + h * 8 + blk] = 1.0
    a[:, AUX_BLK0:AUX_BLK0 + N_SPLIT] = (blk * MOBA_BLOCK)[:, None]
    a[:, AUX_OFF0:AUX_OFF0 + N_SPLIT] = off[:, None]
    a[:, AUX_ROW] = 1.0
    return jnp.asarray(a, BF16)


def _value_aux():
    a = np.zeros((2 * MOBA_BLOCK, LANES), np.float32)
    a[:, 0] = 1.0
    return jnp.asarray(a, BF16)


def _query_aux_rows(slopes):
    a = np.zeros((len(slopes), LANES), np.float32)
    for i, s in enumerate(slopes):
        a[i, AUX_BLK0:AUX_BLK0 + N_SPLIT] = _bf16_pieces(s * LOG2E)
        a[i, AUX_OFF0:AUX_OFF0 + N_SPLIT] = _bf16_pieces(s * LOG2E)
    return jnp.asarray(a)


def _query_aux(crow, slope, j, pen_t=None, pen_lo=0):
    blk = MOBA_BLOCK
    lane = lax.broadcasted_iota(jnp.int32, (blk, LANES), 1)
    t_q = (j * blk).astype(F32) + lax.broadcasted_iota(jnp.int32, (blk, LANES), 0).astype(F32)
    aux = jnp.where(lane == AUX_ROW, (-slope * LOG2E) * t_q, crow)
    if pen_t is not None:
        aux = jnp.where((lane >= pen_lo) & (lane < pen_lo + 8), pen_t, aux)
    return aux.astype(BF16)


def _attend(qaug_ref, k_ref, v_ref, kaux_ref, vaux_ref, s_ref, sown_ref, mrun_ref, acc_ref, j, kcols, vcols):
    n_items = qaug_ref.shape[0]
    blk = MOBA_BLOCK
    causal = (lax.broadcasted_iota(jnp.int32, (blk, blk), 0) >= lax.broadcasted_iota(jnp.int32, (blk, blk), 1))
    own = pl.ds(pl.multiple_of(j * blk, blk), blk)

    def scores(i, rows, kaux):
        return _dot_nt(qaug_ref[i], jnp.concatenate([k_ref[rows, kcols[i]], kaux], axis=1))

    def half_max(s):
        return jnp.maximum(s[:, :LANES], s[:, LANES:])

    kaux_own = kaux_ref[own, :]
    for i in range(n_items):
        s = jnp.where(causal, scores(i, own, kaux_own), NEG_INF)
        sown_ref[i] = s
        mrun_ref[i] = half_max(s)

    def pass_a(n, width):
        rows = pl.ds(pl.multiple_of(n * blk, blk), width * blk)
        kaux_n = kaux_ref[rows, :]
        for i in range(n_items):
            s = scores(i, rows, kaux_n)
            m = mrun_ref[i]
            for w in range(width):
                sw = s[:, w * blk:(w + 1) * blk]
                s_ref[i, n + w] = sw
                m = jnp.maximum(m, half_max(sw))
            mrun_ref[i] = m

    def loop_past(fn):
        def pair(t, c):
            fn(2 * t, 2)
            return c
        lax.fori_loop(0, j // 2, pair, 0)

        @pl.when(j % 2 == 1)
        def _():
            fn(j - 1, 1)

    loop_past(pass_a)

    for i in range(n_items):
        mrun_ref[i] = jnp.broadcast_to(jnp.max(mrun_ref[i], axis=1, keepdims=True), (blk, LANES))

    def probs(i, s):
        m = mrun_ref[i]
        return jnp.concatenate([jnp.exp2(s[:, :LANES] - m), jnp.exp2(s[:, LANES:] - m)], axis=1).astype(BF16)

    def pv(i, p, rows, width):
        v_aug = jnp.concatenate([v_ref[rows, vcols[i]], vaux_ref[:width * blk, :]], axis=1)
        return jnp.dot(p, v_aug, preferred_element_type=F32)

    for i in range(n_items):
        acc_ref[i] = pv(i, probs(i, sown_ref[i]), own, 1)

    def pass_b(n, width):
        rows = pl.ds(pl.multiple_of(n * blk, blk), width * blk)
        for i in range(n_items):
            p = jnp.concatenate([probs(i, s_ref[i, n + w]) for w in range(width)], axis=1)
            acc_ref[i] += pv(i, p, rows, width)

    loop_past(pass_b)

    outs = []
    for i in range(n_items):
        a = acc_ref[i]
        outs.append(a[:, :LANES] * (1.0 / a[:, LANES:LANES + 1]))
    return outs


def _attend_scratch(n_items, nblk):
    blk = MOBA_BLOCK
    return [pltpu.VMEM((n_items, blk, 2 * LANES), BF16),
            pltpu.VMEM((n_items, nblk, blk, blk), F32),
            pltpu.VMEM((n_items, blk, blk), F32),
            pltpu.VMEM((n_items, blk, LANES), F32),
            pltpu.VMEM((n_items, blk, 2 * LANES), F32)]


def _moba_kernel(q_ref, k_ref, v_ref, kaux_ref, vaux_ref, crow_ref, g_ref, o_ref,
                 kmean_ref, kbd_ref, pen_ref, qaug_ref, s_ref, sown_ref, mrun_ref, acc_ref):
    j = pl.program_id(1)
    blk = MOBA_BLOCK
    nblk = s_ref.shape[1]
    slopes = _alibi_slopes(MOBA_HEADS)
    n_slabs = MOBA_WIDTH // LANES

    @pl.when(j == 0)
    def _():
        kmean_ref[...] = jnp.zeros_like(kmean_ref)
        for n in range(nblk):
            kb = k_ref[n * blk:(n + 1) * blk, :].astype(F32)
            kmean_ref[n:n + 1, :] = jnp.sum(kb, axis=0, keepdims=True) * (1.0 / blk)
        col_head = lax.broadcasted_iota(jnp.int32, kmean_ref.shape, 1) // HEAD_DIM
        for h in range(MOBA_HEADS):
            kbd_ref[h * 8:(h + 1) * 8, :] = jnp.where(col_head == h, kmean_ref[...], 0.0)

    pen_ref[...] = jnp.zeros_like(pen_ref)

    @pl.when(j > MOBA_TOPK)
    def _():
        kbd = kbd_ref[...]
        kbd_hi = kbd.astype(BF16)
        kbd_lo = (kbd - kbd_hi.astype(F32)).astype(BF16)
        gate = _dot_nt(kbd_hi, q_ref[...]) + _dot_nt(kbd_lo, q_ref[...])
        n_iota = lax.broadcasted_iota(jnp.int32, (8, blk), 0)
        for h in range(MOBA_HEADS):
            g = gate[h * 8:(h + 1) * 8, :]
            rank = jnp.zeros((8, blk), jnp.int32)
            for m in range(nblk):
                gm = g[m:m + 1, :]
                ahead = (gm > g) | ((gm == g) & (m < n_iota))
                rank = rank + jnp.where(ahead, 1, 0) * (m < j).astype(jnp.int32)
            pen_ref[h * 8:(h + 1) * 8, :] = jnp.where((n_iota < j) & (rank >= MOBA_TOPK), NEG_INF, 0.0)

    pen_t = pen_ref[...].T

    lane = lax.broadcasted_iota(jnp.int32, (blk, LANES), 1)
    for h in range(MOBA_HEADS):
        half = h % 2
        q_slab = q_ref[:, (h // 2) * LANES:(h // 2 + 1) * LANES]
        in_head = (lane >= half * HEAD_DIM) & (lane < (half + 1) * HEAD_DIM)
        aux = _query_aux(crow_ref[h:h + 1, :], slopes[h], j, pen_t, AUX_PEN0 + h * 8)
        qaug_ref[h] = jnp.concatenate([jnp.where(in_head, q_slab, 0).astype(BF16), aux], axis=1)

    cols = [slice((h // 2) * LANES, (h // 2 + 1) * LANES) for h in range(MOBA_HEADS)]
    outs = _attend(qaug_ref, k_ref, v_ref, kaux_ref, vaux_ref, s_ref, sown_ref, mrun_ref, acc_ref, j, cols, cols)
    o = jnp.concatenate([jnp.where(lane < HEAD_DIM, outs[2 * p], outs[2 * p + 1]) for p in range(n_slabs)], axis=1)
    o_ref[...] = _rms(o, g_ref[...]).astype(BF16)


def _moba(qa, ka, va, key_aux, val_aux, g, batch, seq):
    nblk = seq // MOBA_BLOCK
    w = MOBA_WIDTH
    crow = _query_aux_rows(_alibi_slopes(MOBA_HEADS))
    whole = lambda arr: pl.BlockSpec(arr.shape, lambda b, j: (0, 0))
    return pl.pallas_call(
        _moba_kernel,
        grid=(batch, nblk),
        in_specs=[pl.BlockSpec((MOBA_BLOCK, w), lambda b, j: (b * nblk + j, 0)),
                  pl.BlockSpec((seq, w), lambda b, j: (b, 0)),
                  pl.BlockSpec((seq, w), lambda b, j: (b, 0)),
                  whole(key_aux), whole(val_aux), whole(crow), whole(g)],
        out_specs=pl.BlockSpec((MOBA_BLOCK, w), lambda b, j: (b * nblk + j, 0)),
        out_shape=jax.ShapeDtypeStruct((batch * seq, w), BF16),
        scratch_shapes=[pltpu.VMEM((8, w), F32), pltpu.VMEM((8 * MOBA_HEADS, w), F32),
                        pltpu.VMEM((LANES, MOBA_BLOCK), F32)]
        + _attend_scratch(MOBA_HEADS, nblk),
        compiler_params=_params("arbitrary", "arbitrary"),
        name="moba",
    )(qa, ka, va, key_aux, val_aux, crow, g)


def _diff_kernel(q_ref, k_ref, v_ref, kaux_ref, vaux_ref, crow_ref, lam_ref, g_ref, o_ref,
                 qaug_ref, s_ref, sown_ref, mrun_ref, acc_ref, *, lam_init):
    j = pl.program_id(1)
    blk = MOBA_BLOCK
    slopes = _alibi_slopes(DIFF_HEADS)
    lane = lax.broadcasted_iota(jnp.int32, (blk, LANES), 1)

    lv = lam_ref[...]
    lam = (jnp.exp(jnp.sum(lv[0:1] * lv[1:2], axis=1, keepdims=True))
           - jnp.exp(jnp.sum(lv[2:3] * lv[3:4], axis=1, keepdims=True)) + lam_init)

    for i in range(2 * DIFF_HEADS):
        h, c = i // 2, i % 2
        q_slab = q_ref[:, h * LANES:(h + 1) * LANES]
        in_map = (lane >= c * HEAD_DIM) & (lane < (c + 1) * HEAD_DIM)
        aux = _query_aux(crow_ref[i:i + 1, :], slopes[h], j)
        qaug_ref[i] = jnp.concatenate([jnp.where(in_map, q_slab, 0).astype(BF16), aux], axis=1)

    cols = [slice((i // 2) * LANES, (i // 2 + 1) * LANES) for i in range(2 * DIFF_HEADS)]
    outs = _attend(qaug_ref, k_ref, v_ref, kaux_ref, vaux_ref, s_ref, sown_ref, mrun_ref, acc_ref, j, cols, cols)
    for h in range(DIFF_HEADS):
        o = outs[2 * h] - lam * outs[2 * h + 1]
        o_ref[:, h * LANES:(h + 1) * LANES] = (_rms(o, g_ref[...]) * (1.0 - lam_init)).astype(BF16)


def _diff(qd, kd, vd, key_aux, val_aux, lams, g, batch, seq, lam_init):
    nblk = seq // MOBA_BLOCK
    w = DIFF_WIDTH
    crow = _query_aux_rows([s for s in _alibi_slopes(DIFF_HEADS) for _ in range(2)])
    whole = lambda arr: pl.BlockSpec(arr.shape, lambda b, j: (0, 0))
    return pl.pallas_call(
        functools.partial(_diff_kernel, lam_init=lam_init),
        grid=(batch, nblk),
        in_specs=[pl.BlockSpec((MOBA_BLOCK, w), lambda b, j: (b * nblk + j, 0)),
                  pl.BlockSpec((seq, w), lambda b, j: (b, 0)),
                  pl.BlockSpec((seq, w), lambda b, j: (b, 0)),
                  whole(key_aux), whole(val_aux), whole(crow), whole(lams), whole(g)],
        out_specs=pl.BlockSpec((MOBA_BLOCK, w), lambda b, j: (b * nblk + j, 0)),
        out_shape=jax.ShapeDtypeStruct((batch * seq, w), BF16),
        scratch_shapes=_attend_scratch(2 * DIFF_HEADS, nblk),
        compiler_params=_params("arbitrary", "arbitrary"),
        name="diff",
    )(qd, kd, vd, key_aux, val_aux, crow, lams, g)


def _outproj_kernel(a_ref, d_ref, x_ref, wo_ref, g_ref, wq_ref, h_ref, q_ref, *, q_scale):
    wa = a_ref.shape[1]
    h = (x_ref[...]
         + jnp.dot(a_ref[...], wo_ref[:wa, :], preferred_element_type=F32)
         + jnp.dot(d_ref[...], wo_ref[wa:, :], preferred_element_type=F32))
    h_ref[...] = h
    qn = _rms(h, g_ref[...]).astype(BF16)
    q_ref[...] = (jnp.dot(qn, wq_ref[...], preferred_element_type=F32) * q_scale).astype(BF16)


def _outproj(a, d, x2, w_out, g, w_q, q_scale):
    t, dm = x2.shape
    tile = lambda w: pl.BlockSpec((TM_PROJ, w), lambda i: (i, 0))
    whole = lambda arr: pl.BlockSpec(arr.shape, lambda i: (0, 0))
    return pl.pallas_call(
        functools.partial(_outproj_kernel, q_scale=q_scale),
        grid=(t // TM_PROJ,),
        in_specs=[tile(a.shape[1]), tile(d.shape[1]), tile(dm), whole(w_out), whole(g), whole(w_q)],
        out_specs=[tile(dm), tile(dm)],
        out_shape=[jax.ShapeDtypeStruct((t, dm), F32), jax.ShapeDtypeStruct((t, dm), BF16)],
        compiler_params=_params("arbitrary"),
        name="outproj",
    )(a, d, x2, w_out, g, w_q)


def _memkv_kernel(m_ref, g_ref, w_ref, kv_ref):
    mn = _rms(m_ref[...], g_ref[...]).astype(BF16)
    kv_ref[...] = jnp.dot(mn, w_ref[...], preferred_element_type=F32).astype(BF16)


def _memkv(mem2, g, w_kv, mem_len):
    rows, dm = mem2.shape
    return pl.pallas_call(
        _memkv_kernel,
        grid=(rows // mem_len,),
        in_specs=[pl.BlockSpec((mem_len, dm), lambda i: (i, 0)),
                  pl.BlockSpec((1, dm), lambda i: (0, 0)),
                  pl.BlockSpec(w_kv.shape, lambda i: (0, 0))],
        out_specs=pl.BlockSpec((mem_len, w_kv.shape[1]), lambda i: (i, 0)),
        out_shape=jax.ShapeDtypeStruct((rows, w_kv.shape[1]), BF16),
        compiler_params=_params("arbitrary"),
        name="memkv",
    )(mem2, g, w_kv)


def _cross_kernel(q_ref, kv_ref, h_ref, wo_ref, g_ref, wr_ref, br_ref,
                  h2_ref, xn_ref, route_ref, counts_ref, run_ref):
    first = (pl.program_id(0) == 0) & (pl.program_id(1) == 0)

    @pl.when(first)
    def _():
        run_ref[...] = jnp.zeros_like(run_ref)

    tq, dm = h_ref.shape
    dh = dm // MEM_HEADS
    heads = []
    for h in range(MEM_HEADS):
        s = _dot_nt(q_ref[:, h * dh:(h + 1) * dh], kv_ref[:, h * dh:(h + 1) * dh])
        m = jnp.max(s, axis=-1, keepdims=True)
        e = jnp.exp(s - m)
        l = jnp.sum(e, axis=-1, keepdims=True)
        o = jnp.dot(e.astype(BF16), kv_ref[:, dm + h * dh:dm + (h + 1) * dh], preferred_element_type=F32)
        heads.append((o * (1.0 / l)).astype(BF16))
    o = jnp.concatenate(heads, axis=1)
    h2 = h_ref[...] + jnp.dot(o, wo_ref[...], preferred_element_type=F32)
    h2_ref[...] = h2
    xn = _rms(h2, g_ref[...])
    _to_token_major(xn_ref, xn)

    x_hi = xn.astype(BF16)
    x_lo = (xn - x_hi.astype(F32)).astype(BF16)
    hi = jnp.dot(x_hi, wr_ref[...], preferred_element_type=F32)
    lo_hi = jnp.dot(x_lo, wr_ref[:, :ROUTE_COLS], preferred_element_type=F32)
    logits = hi[:, :ROUTE_COLS] + (hi[:, ROUTE_COLS:] + lo_hi) + br_ref[...]
    lane = lax.broadcasted_iota(jnp.int32, logits.shape, 1).astype(F32)
    big = float(ROUTE_COLS)

    def top1(vals):
        v = jnp.max(vals, axis=-1, keepdims=True)
        i = jnp.min(jnp.where(vals == v, lane, big), axis=-1, keepdims=True)
        return v, i

    gl = jnp.where(lane < N_GROUPS, logits, -jnp.inf)
    g_max, g_idx = top1(gl)
    g_w = 1.0 / jnp.sum(jnp.exp(gl - g_max), axis=-1, keepdims=True)
    lo = EXPERT_COL0 + g_idx * EXPERTS_PER_GROUP
    el = jnp.where((lane >= lo) & (lane < lo + EXPERTS_PER_GROUP), logits, -jnp.inf)
    v1, i1 = top1(el)
    v2, i2 = top1(jnp.where(lane == i1, -jnp.inf, el))
    e2 = jnp.exp(v2 - v1)
    w1 = g_w * (1.0 / (1.0 + e2))
    w2 = g_w * (e2 / (1.0 + e2))

    onehot = jnp.where((lane == i1) | (lane == i2), 1.0, 0.0)
    ri = lax.broadcasted_iota(jnp.int32, (tq, tq), 0)
    ci = lax.broadcasted_iota(jnp.int32, (tq, tq), 1)
    earlier = jnp.where(ci < ri, 1.0, 0.0).astype(BF16)
    before = jnp.dot(earlier, onehot.astype(BF16), preferred_element_type=F32) + run_ref[...]
    r1 = jnp.sum(jnp.where(lane == i1, before, 0.0), axis=-1, keepdims=True)
    r2 = jnp.sum(jnp.where(lane == i2, before, 0.0), axis=-1, keepdims=True)
    run_ref[...] = run_ref[...] + jnp.sum(onehot, axis=0, keepdims=True)
    counts_ref[...] = run_ref[...]

    rec = jnp.zeros(logits.shape, F32)
    for col, val in enumerate((i1 - EXPERT_COL0, i2 - EXPERT_COL0, w1, w2, r1, r2)):
        rec = jnp.where(lane == col, val, rec)
    route_ref[...] = rec


def _cross(qc, kv, h1, w_o, g, w_r, b_r, batch, seq, mem_len):
    t, dm = h1.shape
    nt = seq // TQ_CROSS
    tile = lambda w: pl.BlockSpec((TQ_CROSS, w), lambda b, i: (b * nt + i, 0))
    whole = lambda arr: pl.BlockSpec(arr.shape, lambda b, i: (0, 0))
    return pl.pallas_call(
        _cross_kernel,
        grid=(batch, nt),
        in_specs=[tile(dm), pl.BlockSpec((mem_len, kv.shape[1]), lambda b, i: (b, 0)), tile(dm),
                  whole(w_o), whole(g), whole(w_r), whole(b_r)],
        out_specs=[tile(dm), pl.BlockSpec((TQ_CROSS * SUB, LANES), lambda b, i: (b * nt + i, 0)),
                   tile(ROUTE_COLS), pl.BlockSpec((1, ROUTE_COLS), lambda b, i: (0, 0))],
        out_shape=[jax.ShapeDtypeStruct((t, dm), F32), jax.ShapeDtypeStruct((t * SUB, LANES), F32),
                   jax.ShapeDtypeStruct((t, ROUTE_COLS), F32), jax.ShapeDtypeStruct((1, ROUTE_COLS), F32)],
        scratch_shapes=[pltpu.VMEM((1, ROUTE_COLS), F32)],
        compiler_params=_params("arbitrary", "arbitrary"),
        name="cross",
    )(qc, kv, h1, w_o, g, w_r, b_r)


SUB = 8


def _to_token_major(ref, x):
    rows = x.shape[0]
    for c in range(x.shape[1] // LANES):
        ref[pl.ds(c, rows, stride=SUB), :] = x[:, c * LANES:(c + 1) * LANES]


def _from_token_major(ref, rows):
    n_chunks = ref.shape[0] // rows
    return jnp.concatenate([ref[pl.ds(c, rows, stride=SUB), :] for c in range(n_chunks)], axis=1)


def _experts_kernel(te_ref, nused_ref, dst_ref, xn_ref, wgu_ref, wd_ref, yo_ref, xbuf, ybuf, gsem, ssem, *, n_tok):
    i = pl.program_id(0)
    last = pl.num_programs(0) - 1
    n_used = nused_ref[0]
    tm = xbuf.shape[1] // SUB
    ff = wd_ref.shape[0]
    slot = i % 2
    token_rows = lambda tok: pl.ds(pl.multiple_of(tok * SUB, SUB), SUB)

    def for_each_slot(tile, fn):
        for q in range(tm // LANES):
            for lane in range(LANES):
                fn(dst_ref[tile * (tm // LANES) + q, lane], q * LANES + lane)

    def gather(tile, buf):
        for_each_slot(tile, lambda dst, r: pltpu.make_async_copy(
            xn_ref.at[token_rows(dst & (n_tok - 1))], xbuf.at[buf, token_rows(r)], gsem.at[buf]).start())

    def scatter(tile, buf):
        for_each_slot(tile, lambda dst, r: pltpu.make_async_copy(
            ybuf.at[buf, token_rows(r)], yo_ref.at[token_rows(dst)], ssem.at[buf]).start())

    def wait_gather(buf):
        pltpu.make_async_copy(xn_ref.at[pl.ds(0, tm * SUB)], xbuf.at[buf], gsem.at[buf]).wait()

    def wait_scatter(buf):
        pltpu.make_async_copy(ybuf.at[buf], yo_ref.at[pl.ds(0, tm * SUB)], ssem.at[buf]).wait()

    @pl.when((i == 0) & (n_used > 0))
    def _():
        gather(0, 0)

    @pl.when((i >= 2) & (i - 2 < n_used))
    def _():
        wait_scatter(slot)

    @pl.when(i < n_used)
    def _():
        wait_gather(slot)
        gather(jnp.minimum(i + 1, n_used - 1), 1 - slot)
        x = _from_token_major(xbuf.at[slot], tm).astype(BF16)
        gu = jnp.dot(x, wgu_ref[...], preferred_element_type=F32)
        g = gu[:, :ff]
        hh = (g * jax.nn.sigmoid(g)) * gu[:, ff:]
        _to_token_major(ybuf.at[slot], jnp.dot(hh.astype(BF16), wd_ref[...], preferred_element_type=F32))
        scatter(i, slot)

        @pl.when(i + 1 == n_used)
        def _():
            wait_gather(1 - slot)

    @pl.when(i >= n_used)
    def _():
        ybuf[slot] = jnp.zeros(ybuf.shape[1:], F32)
        fill = pltpu.make_async_copy(
            ybuf.at[slot], yo_ref.at[pl.ds(pl.multiple_of(i * (tm * SUB), tm * SUB), tm * SUB)], ssem.at[slot])
        fill.start()
        fill.wait()

    @pl.when(i == last)
    def _():
        @pl.when((i >= 1) & (i - 1 < n_used))
        def _():
            wait_scatter(1 - slot)

        @pl.when(i < n_used)
        def _():
            wait_scatter(slot)


def _experts(tile_expert, n_used, dst, xn_tm, wgu, wd, tm):
    n_tok = xn_tm.shape[0] // SUB
    n_tiles = dst.shape[0] * LANES // tm
    _, dm, ff2 = wgu.shape
    assert n_tok & (n_tok - 1) == 0, "source token = dst & (n_tok - 1) needs a power-of-two token count"
    assert dm == SUB * LANES and tm % LANES == 0 and dst.shape[1] == LANES
    return pl.pallas_call(
        functools.partial(_experts_kernel, n_tok=n_tok),
        grid_spec=pltpu.PrefetchScalarGridSpec(
            num_scalar_prefetch=3,
            grid=(n_tiles,),
            in_specs=[pl.BlockSpec(memory_space=pl.ANY),
                      pl.BlockSpec((None, dm, ff2), lambda i, te, nu, ds: (te[i], 0, 0)),
                      pl.BlockSpec((None, ff2 // 2, dm), lambda i, te, nu, ds: (te[i], 0, 0))],
            out_specs=pl.BlockSpec(memory_space=pl.ANY),
            scratch_shapes=[pltpu.VMEM((2, tm * SUB, LANES), F32), pltpu.VMEM((2, tm * SUB, LANES), F32),
                            pltpu.SemaphoreType.DMA((2,)), pltpu.SemaphoreType.DMA((2,))]),
        out_shape=jax.ShapeDtypeStruct((n_tiles * tm * SUB, LANES), F32),
        compiler_params=_params("arbitrary"),
        name="experts",
    )(tile_expert, n_used, dst, xn_tm, wgu, wd)


def _combine_kernel(h2_ref, route_ref, y0_ref, y1_ref, g_ref, o_ref):
    rows = h2_ref.shape[0]
    rec = route_ref[...]
    h = (h2_ref[...] + rec[:, 2:3] * _from_token_major(y0_ref, rows)
         + rec[:, 3:4] * _from_token_major(y1_ref, rows))
    o_ref[...] = _rms(h, g_ref[...])


def _combine(h2, route, yo, g):
    t, dm = h2.shape
    nt = t // TM_PROJ
    tile = lambda w: pl.BlockSpec((TM_PROJ, w), lambda i: (i, 0))
    return pl.pallas_call(
        _combine_kernel,
        grid=(nt,),
        in_specs=[tile(dm), tile(ROUTE_COLS),
                  pl.BlockSpec((TM_PROJ * SUB, LANES), lambda i: (i, 0)),
                  pl.BlockSpec((TM_PROJ * SUB, LANES), lambda i: (i + nt, 0)),
                  pl.BlockSpec((1, dm), lambda i: (0, 0))],
        out_specs=tile(dm),
        out_shape=jax.ShapeDtypeStruct((t, dm), F32),
        compiler_params=_params("arbitrary"),
        name="combine",
    )(h2, route, yo, yo, g)


def kernel(x, mem, norm_mix, w_in, lambda_q1, lambda_k1, lambda_q2, lambda_k2, diff_subln, norm_moba_out, w_out, norm_mem_q, norm_mem_kv, w_mem_q, w_mem_kv, w_mem_o, norm_ffn, w_router_group, b_router_group, w_router_expert, b_router_expert, w_expert_gate, w_expert_up, w_expert_down, norm_final):
    batch, seq, dm = x.shape
    mem_len = mem.shape[1]
    depth = w_in.shape[0]
    t = batch * seq
    assert seq % MOBA_BLOCK == 0 and seq % TQ_CROSS == 0 and t % TM_PROJ == 0

    h = x.reshape(t, dm)
    mem2 = mem.reshape(batch * mem_len, dm)
    row = lambda v: v.reshape(1, -1).astype(F32)
    for l in range(depth):
        lam_init = 0.8 - 0.6 * math.exp(-0.3 * l)
        q_scale = HEAD_DIM ** -0.5 * LOG2E
        qa, ka, va, qd, kd, vd = _inproj(h, row(norm_mix[l]), w_in[l].astype(BF16),
                                         (q_scale, 1.0, 1.0, q_scale, 1.0, 1.0))
        key_aux, val_aux = _key_aux(seq), _value_aux()
        a = _moba(qa, ka, va, key_aux, val_aux, row(norm_moba_out[l]), batch, seq)
        lams = jnp.stack([lambda_q1[l], lambda_k1[l], lambda_q2[l], lambda_k2[l]]).astype(F32)
        d = _diff(qd, kd, vd, key_aux, val_aux, lams, row(diff_subln[l]), batch, seq, lam_init)
        h1, qc = _outproj(a, d, h, w_out[l].astype(BF16), row(norm_mem_q[l]), w_mem_q[l].astype(BF16),
                          (dm // MEM_HEADS) ** -0.5)
        kv = _memkv(mem2, row(norm_mem_kv[l]), w_mem_kv[l].astype(BF16), mem_len)

        pad = ROUTE_COLS - N_GROUPS - N_EXPERTS
        w_r = jnp.pad(jnp.concatenate([w_router_group[l], w_router_expert[l]], axis=1).astype(F32), ((0, 0), (0, pad)))
        w_r_hi = w_r.astype(BF16)
        w_r = jnp.concatenate([w_r_hi, (w_r - w_r_hi.astype(F32)).astype(BF16)], axis=1)
        b_r = jnp.pad(jnp.concatenate([b_router_group[l], b_router_expert[l]]).astype(F32), (0, pad)).reshape(1, -1)
        h2, xn, route, counts = _cross(qc, kv, h1, w_mem_o[l].astype(BF16), row(norm_ffn[l]), w_r, b_r,
                                       batch, seq, mem_len)

        tm = TM_EXPERT
        n_tiles = (2 * t) // tm + N_EXPERTS
        cnt = counts[0, EXPERT_COL0:EXPERT_COL0 + N_EXPERTS].astype(jnp.int32)
        padded = ((cnt + tm - 1) // tm) * tm
        ends = jnp.cumsum(padded)
        offs = ends - padded
        eidx = route[:, 0:2].astype(jnp.int32)
        pos = offs[eidx] + route[:, 4:6].astype(jnp.int32)
        tile_start = jnp.arange(n_tiles, dtype=jnp.int32) * tm
        tile_expert = jnp.minimum(jnp.sum((ends[None, :] <= tile_start[:, None]).astype(jnp.int32), axis=1),
                                  N_EXPERTS - 1)
        n_used = (ends[-1:] // tm).astype(jnp.int32)
        slot = jnp.arange(n_tiles * tm, dtype=jnp.int32)
        slot_expert = jnp.repeat(tile_expert, tm)
        pad_rank = slot - offs[slot_expert] - cnt[slot_expert]
        pads_before = jnp.cumsum(padded - cnt) - (padded - cnt)
        dummy = 2 * t + pads_before[slot_expert] + jnp.clip(pad_rank, 0, tm - 1)
        real_row = jnp.arange(t, dtype=jnp.int32)[:, None] + jnp.array([0, t], jnp.int32)[None, :]
        dst = dummy.at[pos.reshape(-1)].set(real_row.reshape(-1), unique_indices=True,
                                            mode="promise_in_bounds").reshape(-1, LANES)

        wgu = jnp.concatenate([w_expert_gate[l], w_expert_up[l]], axis=2).astype(BF16)
        yo = _experts(tile_expert, n_used, dst, xn, wgu, w_expert_down[l].astype(BF16), tm)
        assert depth == 1
        h = _combine(h2, route, yo, row(norm_final))
    return h.reshape(batch, seq, dm)
```

```python
import functools
import math

import jax
import jax.numpy as jnp
import numpy as np
from jax import lax
from jax.experimental import pallas as pl
from jax.experimental.pallas import tpu as pltpu

F32 = jnp.float32
BF16 = jnp.bfloat16

HEAD_DIM = 64
MOBA_HEADS = 8
MOBA_WIDTH = MOBA_HEADS * HEAD_DIM
MOBA_BLOCK = 256
MOBA_TOPK = 3
DIFF_HEADS = 4
DIFF_V_DIM = 2 * HEAD_DIM
DIFF_WIDTH = DIFF_HEADS * DIFF_V_DIM
MEM_HEADS = 4
N_GROUPS = 4
EXPERTS_PER_GROUP = 8
N_EXPERTS = N_GROUPS * EXPERTS_PER_GROUP
RMS_EPS = 1e-6
NEG_INF = -1e30
LOG2E = math.log2(math.e)
LANES = 128
ROUTE_COLS = LANES
EXPERT_COL0 = N_GROUPS

TM_PROJ = 512
TQ_ATTN = MOBA_BLOCK
TQ_CROSS = 512
TM_EXPERT = 256
VMEM_LIMIT = 56 * 1024 * 1024


def _alibi_slopes(n):
    return [2.0 ** (-8.0 * (i + 1) / n) for i in range(n)]


def _rms(x, g):
    y = x * lax.rsqrt(jnp.mean(x * x, axis=-1, keepdims=True) + RMS_EPS)
    return y * g


def _dot_nt(a, b):
    return lax.dot_general(a, b, (((1,), (1,)), ((), ())), preferred_element_type=F32)


def _params(*sem):
    return pltpu.CompilerParams(dimension_semantics=sem, vmem_limit_bytes=VMEM_LIMIT)


def _inproj_kernel(x_ref, g_ref, w_ref, *out_refs, scales):
    xn = _rms(x_ref[...], g_ref[...]).astype(BF16)
    width = out_refs[0].shape[1]
    for i, o_ref in enumerate(out_refs):
        y = jnp.dot(xn, w_ref[:, i * width:(i + 1) * width], preferred_element_type=F32)
        o_ref[...] = (y if scales[i] == 1.0 else y * scales[i]).astype(BF16)


def _inproj(x2, g, w, scales):
    t, d = x2.shape
    n_out = w.shape[1] // MOBA_WIDTH
    return pl.pallas_call(
        functools.partial(_inproj_kernel, scales=scales),
        grid=(t // TM_PROJ,),
        in_specs=[pl.BlockSpec((TM_PROJ, d), lambda i: (i, 0)),
                  pl.BlockSpec((1, d), lambda i: (0, 0)),
                  pl.BlockSpec(w.shape, lambda i: (0, 0))],
        out_specs=[pl.BlockSpec((TM_PROJ, MOBA_WIDTH), lambda i: (i, 0))] * n_out,
        out_shape=[jax.ShapeDtypeStruct((t, MOBA_WIDTH), BF16)] * n_out,
        compiler_params=_params("arbitrary"),
        name="inproj",
    )(x2, g, w)


AUX_PEN0 = 0
AUX_BLK0 = 64
AUX_OFF0 = 68
AUX_ROW = 72
N_SPLIT = 4


def _bf16_pieces(c):
    pieces, rem = [], np.float64(c)
    for _ in range(N_SPLIT):
        p = np.float64(np.asarray(rem).astype(BF16))
        pieces.append(p)
        rem = rem - p
    return pieces


def _key_aux(seq):
    pos = np.arange(seq)
    blk, off = pos // MOBA_BLOCK, pos % MOBA_BLOCK
    assert blk.max() < 8, "penalty lanes hold 8 blocks per head"
    a = np.zeros((seq, LANES), np.float32)
    for h in range(MOBA_HEADS):
        a[pos, AUX_PEN0 + h * 8 + blk] = 1.0
    a[:, AUX_BLK0:AUX_BLK0 + N_SPLIT] = (blk * MOBA_BLOCK)[:, None]
    a[:, AUX_OFF0:AUX_OFF0 + N_SPLIT] = off[:, None]
    a[:, AUX_ROW] = 1.0
    return jnp.asarray(a, BF16)


def _value_aux():
    a = np.zeros((2 * MOBA_BLOCK, LANES), np.float32)
    a[:, 0] = 1.0
    return jnp.asarray(a, BF16)


def _query_aux_rows(slopes):
    a = np.zeros((len(slopes), LANES), np.float32)
    for i, s in enumerate(slopes):
        a[i, AUX_BLK0:AUX_BLK0 + N_SPLIT] = _bf16_pieces(s * LOG2E)
        a[i, AUX_OFF0:AUX_OFF0 + N_SPLIT] = _bf16_pieces(s * LOG2E)
    return jnp.asarray(a)


def _query_aux(crow, slope, j, pen_t=None, pen_lo=0):
    blk = MOBA_BLOCK
    lane = lax.broadcasted_iota(jnp.int32, (blk, LANES), 1)
    t_q = (j * blk).astype(F32) + lax.broadcasted_iota(jnp.int32, (blk, LANES), 0).astype(F32)
    aux = jnp.where(lane == AUX_ROW, (-slope * LOG2E) * t_q, crow)
    if pen_t is not None:
        aux = jnp.where((lane >= pen_lo) & (lane < pen_lo + 8), pen_t, aux)
    return aux.astype(BF16)


def _attend(qaug_ref, k_ref, v_ref, kaux_ref, vaux_ref, s_ref, sown_ref, mrun_ref, acc_ref, j, kcols, vcols):
    n_items = qaug_ref.shape[0]
    blk = MOBA_BLOCK
    causal = (lax.broadcasted_iota(jnp.int32, (blk, blk), 0) >= lax.broadcasted_iota(jnp.int32, (blk, blk), 1))
    own = pl.ds(pl.multiple_of(j * blk, blk), blk)

    def scores(i, rows, kaux):
        return _dot_nt(qaug_ref[i], jnp.concatenate([k_ref[rows, kcols[i]], kaux], axis=1))

    def half_max(s):
        return jnp.maximum(s[:, :LANES], s[:, LANES:])

    kaux_own = kaux_ref[own, :]
    for i in range(n_items):
        s = jnp.where(causal, scores(i, own, kaux_own), NEG_INF)
        sown_ref[i] = s
        mrun_ref[i] = half_max(s)

    def pass_a(n, width):
        rows = pl.ds(pl.multiple_of(n * blk, blk), width * blk)
        kaux_n = kaux_ref[rows, :]
        for i in range(n_items):
            s = scores(i, rows, kaux_n)
            m = mrun_ref[i]
            for w in range(width):
                sw = s[:, w * blk:(w + 1) * blk]
                s_ref[i, n + w] = sw
                m = jnp.maximum(m, half_max(sw))
            mrun_ref[i] = m

    def loop_past(fn):
        def pair(t, c):
            fn(2 * t, 2)
            return c
        lax.fori_loop(0, j // 2, pair, 0)

        @pl.when(j % 2 == 1)
        def _():
            fn(j - 1, 1)

    loop_past(pass_a)

    for i in range(n_items):
        mrun_ref[i] = jnp.broadcast_to(jnp.max(mrun_ref[i], axis=1, keepdims=True), (blk, LANES))

    def probs(i, s):
        m = mrun_ref[i]
        return jnp.concatenate([jnp.exp2(s[:, :LANES] - m), jnp.exp2(s[:, LANES:] - m)], axis=1).astype(BF16)

    def pv(i, p, rows, width):
        v_aug = jnp.concatenate([v_ref[rows, vcols[i]], vaux_ref[:width * blk, :]], axis=1)
        return jnp.dot(p, v_aug, preferred_element_type=F32)

    for i in range(n_items):
        acc_ref[i] = pv(i, probs(i, sown_ref[i]), own, 1)

    def pass_b(n, width):
        rows = pl.ds(pl.multiple_of(n * blk, blk), width * blk)
        for i in range(n_items):
            p = jnp.concatenate([probs(i, s_ref[i, n + w]) for w in range(width)], axis=1)
            acc_ref[i] += pv(i, p, rows, width)

    loop_past(pass_b)

    outs = []
    for i in range(n_items):
        a = acc_ref[i]
        outs.append(a[:, :LANES] * (1.0 / a[:, LANES:LANES + 1]))
    return outs


def _attend_scratch(n_items, nblk):
    blk = MOBA_BLOCK
    return [pltpu.VMEM((n_items, blk, 2 * LANES), BF16),
            pltpu.VMEM((n_items, nblk, blk, blk), F32),
            pltpu.VMEM((n_items, blk, blk), F32),
            pltpu.VMEM((n_items, blk, LANES), F32),
            pltpu.VMEM((n_items, blk, 2 * LANES), F32)]


def _moba_kernel(q_ref, k_ref, v_ref, kaux_ref, vaux_ref, crow_ref, g_ref, o_ref,
                 kmean_ref, kbd_ref, pen_ref, qaug_ref, s_ref, sown_ref, mrun_ref, acc_ref):
    j = pl.program_id(1)
    blk = MOBA_BLOCK
    nblk = s_ref.shape[1]
    slopes = _alibi_slopes(MOBA_HEADS)
    n_slabs = MOBA_WIDTH // LANES

    @pl.when(j == 0)
    def _():
        kmean_ref[...] = jnp.zeros_like(kmean_ref)
        for n in range(nblk):
            kb = k_ref[n * blk:(n + 1) * blk, :].astype(F32)
            kmean_ref[n:n + 1, :] = jnp.sum(kb, axis=0, keepdims=True) * (1.0 / blk)
        col_head = lax.broadcasted_iota(jnp.int32, kmean_ref.shape, 1) // HEAD_DIM
        for h in range(MOBA_HEADS):
            kbd_ref[h * 8:(h + 1) * 8, :] = jnp.where(col_head == h, kmean_ref[...], 0.0)

    pen_ref[...] = jnp.zeros_like(pen_ref)

    @pl.when(j > MOBA_TOPK)
    def _():
        kbd = kbd_ref[...]
        kbd_hi = kbd.astype(BF16)
        kbd_lo = (kbd - kbd_hi.astype(F32)).astype(BF16)
        gate = _dot_nt(kbd_hi, q_ref[...]) + _dot_nt(kbd_lo, q_ref[...])
        n_iota = lax.broadcasted_iota(jnp.int32, (8, blk), 0)
        for h in range(MOBA_HEADS):
            g = gate[h * 8:(h + 1) * 8, :]
            rank = jnp.zeros((8, blk), jnp.int32)
            for m in range(nblk):
                gm = g[m:m + 1, :]
                ahead = (gm > g) | ((gm == g) & (m < n_iota))
                rank = rank + jnp.where(ahead, 1, 0) * (m < j).astype(jnp.int32)
            pen_ref[h * 8:(h + 1) * 8, :] = jnp.where((n_iota < j) & (rank >= MOBA_TOPK), NEG_INF, 0.0)

    pen_t = pen_ref[...].T

    lane = lax.broadcasted_iota(jnp.int32, (blk, LANES), 1)
    for h in range(MOBA_HEADS):
        half = h % 2
        q_slab = q_ref[:, (h // 2) * LANES:(h // 2 + 1) * LANES]
        in_head = (lane >= half * HEAD_DIM) & (lane < (half + 1) * HEAD_DIM)
        aux = _query_aux(crow_ref[h:h + 1, :], slopes[h], j, pen_t, AUX_PEN0 + h * 8)
        qaug_ref[h] = jnp.concatenate([jnp.where(in_head, q_slab, 0).astype(BF16), aux], axis=1)

    cols = [slice((h // 2) * LANES, (h // 2 + 1) * LANES) for h in range(MOBA_HEADS)]
    outs = _attend(qaug_ref, k_ref, v_ref, kaux_ref, vaux_ref, s_ref, sown_ref, mrun_ref, acc_ref, j, cols, cols)
    o = jnp.concatenate([jnp.where(lane < HEAD_DIM, outs[2 * p], outs[2 * p + 1]) for p in range(n_slabs)], axis=1)
    o_ref[...] = _rms(o, g_ref[...]).astype(BF16)


def _moba(qa, ka, va, key_aux, val_aux, g, batch, seq):
    nblk = seq // MOBA_BLOCK
    w = MOBA_WIDTH
    crow = _query_aux_rows(_alibi_slopes(MOBA_HEADS))
    whole = lambda arr: pl.BlockSpec(arr.shape, lambda b, j: (0, 0))
    return pl.pallas_call(
        _moba_kernel,
        grid=(batch, nblk),
        in_specs=[pl.BlockSpec((MOBA_BLOCK, w), lambda b, j: (b * nblk + j, 0)),
                  pl.BlockSpec((seq, w), lambda b, j: (b, 0)),
                  pl.BlockSpec((seq, w), lambda b, j: (b, 0)),
                  whole(key_aux), whole(val_aux), whole(crow), whole(g)],
        out_specs=pl.BlockSpec((MOBA_BLOCK, w), lambda b, j: (b * nblk + j, 0)),
        out_shape=jax.ShapeDtypeStruct((batch * seq, w), BF16),
        scratch_shapes=[pltpu.VMEM((8, w), F32), pltpu.VMEM((8 * MOBA_HEADS, w), F32),
                        pltpu.VMEM((LANES, MOBA_BLOCK), F32)]
        + _attend_scratch(MOBA_HEADS, nblk),
        compiler_params=_params("arbitrary", "arbitrary"),
        name="moba",
    )(qa, ka, va, key_aux, val_aux, crow, g)


def _diff_kernel(q_ref, k_ref, v_ref, kaux_ref, vaux_ref, crow_ref, lam_ref, g_ref, o_ref,
                 qaug_ref, s_ref, sown_ref, mrun_ref, acc_ref, *, lam_init):
    j = pl.program_id(1)
    blk = MOBA_BLOCK
    slopes = _alibi_slopes(DIFF_HEADS)
    lane = lax.broadcasted_iota(jnp.int32, (blk, LANES), 1)

    lv = lam_ref[...]
    lam = (jnp.exp(jnp.sum(lv[0:1] * lv[1:2], axis=1, keepdims=True))
           - jnp.exp(jnp.sum(lv[2:3] * lv[3:4], axis=1, keepdims=True)) + lam_init)

    for i in range(2 * DIFF_HEADS):
        h, c = i // 2, i % 2
        q_slab = q_ref[:, h * LANES:(h + 1) * LANES]
        in_map = (lane >= c * HEAD_DIM) & (lane < (c + 1) * HEAD_DIM)
        aux = _query_aux(crow_ref[i:i + 1, :], slopes[h], j)
        qaug_ref[i] = jnp.concatenate([jnp.where(in_map, q_slab, 0).astype(BF16), aux], axis=1)

    cols = [slice((i // 2) * LANES, (i // 2 + 1) * LANES) for i in range(2 * DIFF_HEADS)]
    outs = _attend(qaug_ref, k_ref, v_ref, kaux_ref, vaux_ref, s_ref, sown_ref, mrun_ref, acc_ref, j, cols, cols)
    for h in range(DIFF_HEADS):
        o = outs[2 * h] - lam * outs[2 * h + 1]
        o_ref[:, h * LANES:(h + 1) * LANES] = (_rms(o, g_ref[...]) * (1.0 - lam_init)).astype(BF16)


def _diff(qd, kd, vd, key_aux, val_aux, lams, g, batch, seq, lam_init):
    nblk = seq // MOBA_BLOCK
    w = DIFF_WIDTH
    crow = _query_aux_rows([s for s in _alibi_slopes(DIFF_HEADS) for _ in range(2)])
    whole = lambda arr: pl.BlockSpec(arr.shape, lambda b, j: (0, 0))
    return pl.pallas_call(
        functools.partial(_diff_kernel, lam_init=lam_init),
        grid=(batch, nblk),
        in_specs=[pl.BlockSpec((MOBA_BLOCK, w), lambda b, j: (b * nblk + j, 0)),
                  pl.BlockSpec((seq, w), lambda b, j: (b, 0)),
                  pl.BlockSpec((seq, w), lambda b, j: (b, 0)),
                  whole(key_aux), whole(val_aux), whole(crow), whole(lams), whole(g)],
        out_specs=pl.BlockSpec((MOBA_BLOCK, w), lambda b, j: (b * nblk + j, 0)),
        out_shape=jax.ShapeDtypeStruct((batch * seq, w), BF16),
        scratch_shapes=_attend_scratch(2 * DIFF_HEADS, nblk),
        compiler_params=_params("arbitrary", "arbitrary"),
        name="diff",
    )(qd, kd, vd, key_aux, val_aux, crow, lams, g)


def _outproj_kernel(a_ref, d_ref, x_ref, wo_ref, g_ref, wq_ref, h_ref, q_ref, *, q_scale):
    wa = a_ref.shape[1]
    h = (x_ref[...]
         + jnp.dot(a_ref[...], wo_ref[:wa, :], preferred_element_type=F32)
         + jnp.dot(d_ref[...], wo_ref[wa:, :], preferred_element_type=F32))
    h_ref[...] = h
    qn = _rms(h, g_ref[...]).astype(BF16)
    q_ref[...] = (jnp.dot(qn, wq_ref[...], preferred_element_type=F32) * q_scale).astype(BF16)


def _outproj(a, d, x2, w_out, g, w_q, q_scale):
    t, dm = x2.shape
    tile = lambda w: pl.BlockSpec((TM_PROJ, w), lambda i: (i, 0))
    whole = lambda arr: pl.BlockSpec(arr.shape, lambda i: (0, 0))
    return pl.pallas_call(
        functools.partial(_outproj_kernel, q_scale=q_scale),
        grid=(t // TM_PROJ,),
        in_specs=[tile(a.shape[1]), tile(d.shape[1]), tile(dm), whole(w_out), whole(g), whole(w_q)],
        out_specs=[tile(dm), tile(dm)],
        out_shape=[jax.ShapeDtypeStruct((t, dm), F32), jax.ShapeDtypeStruct((t, dm), BF16)],
        compiler_params=_params("arbitrary"),
        name="outproj",
    )(a, d, x2, w_out, g, w_q)


def _memkv_kernel(m_ref, g_ref, w_ref, kv_ref):
    mn = _rms(m_ref[...], g_ref[...]).astype(BF16)
    kv_ref[...] = jnp.dot(mn, w_ref[...], preferred_element_type=F32).astype(BF16)


def _memkv(mem2, g, w_kv, mem_len):
    rows, dm = mem2.shape
    return pl.pallas_call(
        _memkv_kernel,
        grid=(rows // mem_len,),
        in_specs=[pl.BlockSpec((mem_len, dm), lambda i: (i, 0)),
                  pl.BlockSpec((1, dm), lambda i: (0, 0)),
                  pl.BlockSpec(w_kv.shape, lambda i: (0, 0))],
        out_specs=pl.BlockSpec((mem_len, w_kv.shape[1]), lambda i: (i, 0)),
        out_shape=jax.ShapeDtypeStruct((rows, w_kv.shape[1]), BF16),
        compiler_params=_params("arbitrary"),
        name="memkv",
    )(mem2, g, w_kv)


def _cross_kernel(q_ref, kv_ref, h_ref, wo_ref, g_ref, wr_ref, br_ref,
                  h2_ref, xn_ref, route_ref, counts_ref, run_ref):
    first = (pl.program_id(0) == 0) & (pl.program_id(1) == 0)

    @pl.when(first)
    def _():
        run_ref[...] = jnp.zeros_like(run_ref)

    tq, dm = h_ref.shape
    dh = dm // MEM_HEADS
    heads = []
    for h in range(MEM_HEADS):
        s = _dot_nt(q_ref[:, h * dh:(h + 1) * dh], kv_ref[:, h * dh:(h + 1) * dh])
        m = jnp.max(s, axis=-1, keepdims=True)
        e = jnp.exp(s - m)
        l = jnp.sum(e, axis=-1, keepdims=True)
        o = jnp.dot(e.astype(BF16), kv_ref[:, dm + h * dh:dm + (h + 1) * dh], preferred_element_type=F32)
        heads.append((o * (1.0 / l)).astype(BF16))
    o = jnp.concatenate(heads, axis=1)
    h2 = h_ref[...] + jnp.dot(o, wo_ref[...], preferred_element_type=F32)
    h2_ref[...] = h2
    xn = _rms(h2, g_ref[...])
    _to_token_major(xn_ref, xn)

    x_hi = xn.astype(BF16)
    x_lo = (xn - x_hi.astype(F32)).astype(BF16)
    hi = jnp.dot(x_hi, wr_ref[...], preferred_element_type=F32)
    lo_hi = jnp.dot(x_lo, wr_ref[:, :ROUTE_COLS], preferred_element_type=F32)
    logits = hi[:, :ROUTE_COLS] + (hi[:, ROUTE_COLS:] + lo_hi) + br_ref[...]
    lane = lax.broadcasted_iota(jnp.int32, logits.shape, 1).astype(F32)
    big = float(ROUTE_COLS)

    def top1(vals):
        v = jnp.max(vals, axis=-1, keepdims=True)
        i = jnp.min(jnp.where(vals == v, lane, big), axis=-1, keepdims=True)
        return v, i

    gl = jnp.where(lane < N_GROUPS, logits, -jnp.inf)
    g_max, g_idx = top1(gl)
    g_w = 1.0 / jnp.sum(jnp.exp(gl - g_max), axis=-1, keepdims=True)
    lo = EXPERT_COL0 + g_idx * EXPERTS_PER_GROUP
    el = jnp.where((lane >= lo) & (lane < lo + EXPERTS_PER_GROUP), logits, -jnp.inf)
    v1, i1 = top1(el)
    v2, i2 = top1(jnp.where(lane == i1, -jnp.inf, el))
    e2 = jnp.exp(v2 - v1)
    w1 = g_w * (1.0 / (1.0 + e2))
    w2 = g_w * (e2 / (1.0 + e2))

    onehot = jnp.where((lane == i1) | (lane == i2), 1.0, 0.0)
    run_ref[...] = run_ref[...] + jnp.sum(onehot, axis=0, keepdims=True)
    counts_ref[...] = run_ref[...]

    rec = jnp.zeros(logits.shape, F32)
    for col, val in enumerate((i1 - EXPERT_COL0, i2 - EXPERT_COL0, w1, w2)):
        rec = jnp.where(lane == col, val, rec)
    route_ref[...] = rec


def _cross(qc, kv, h1, w_o, g, w_r, b_r, batch, seq, mem_len):
    t, dm = h1.shape
    nt = seq // TQ_CROSS
    tile = lambda w: pl.BlockSpec((TQ_CROSS, w), lambda b, i: (b * nt + i, 0))
    whole = lambda arr: pl.BlockSpec(arr.shape, lambda b, i: (0, 0))
    return pl.pallas_call(
        _cross_kernel,
        grid=(batch, nt),
        in_specs=[tile(dm), pl.BlockSpec((mem_len, kv.shape[1]), lambda b, i: (b, 0)), tile(dm),
                  whole(w_o), whole(g), whole(w_r), whole(b_r)],
        out_specs=[tile(dm), pl.BlockSpec((TQ_CROSS * SUB, LANES), lambda b, i: (b * nt + i, 0)),
                   tile(ROUTE_COLS), pl.BlockSpec((1, ROUTE_COLS), lambda b, i: (0, 0))],
        out_shape=[jax.ShapeDtypeStruct((t, dm), F32), jax.ShapeDtypeStruct((t * SUB, LANES), F32),
                   jax.ShapeDtypeStruct((t, ROUTE_COLS), F32), jax.ShapeDtypeStruct((1, ROUTE_COLS), F32)],
        scratch_shapes=[pltpu.VMEM((1, ROUTE_COLS), F32)],
        compiler_params=_params("arbitrary", "arbitrary"),
        name="cross",
    )(qc, kv, h1, w_o, g, w_r, b_r)


SUB = 8


def _to_token_major(ref, x):
    rows = x.shape[0]
    for c in range(x.shape[1] // LANES):
        ref[pl.ds(c, rows, stride=SUB), :] = x[:, c * LANES:(c + 1) * LANES]


def _from_token_major(ref, rows):
    n_chunks = ref.shape[0] // rows
    return jnp.concatenate([ref[pl.ds(c, rows, stride=SUB), :] for c in range(n_chunks)], axis=1)


def _experts_kernel(te_ref, nused_ref, cs_ref, nv_ref, sorted_ref, xn_ref, wgu_ref, wd_ref, yo_ref,
                    xbuf, ybuf, gsem, ssem, *, n_tok):
    i = pl.program_id(0)
    last = pl.num_programs(0) - 1
    n_used = nused_ref[0]
    tm = xbuf.shape[1] // SUB
    ff = wd_ref.shape[0]
    slot = i % 2
    token_rows = lambda tok: pl.ds(pl.multiple_of(tok * SUB, SUB), SUB)
    out_row = lambda tile, s: sorted_ref[cs_ref[tile] + s] & (2 * n_tok - 1)

    def for_each_slot(tile, fn):
        @pl.when(nv_ref[tile] == tm)
        def _():
            for s in range(tm):
                fn(s)

        @pl.when(nv_ref[tile] < tm)
        def _():
            def body(s, c):
                fn(s)
                return c
            lax.fori_loop(0, nv_ref[tile], body, 0)

    def gather_copy(tile, buf, s):
        return pltpu.make_async_copy(xn_ref.at[token_rows(out_row(tile, s) & (n_tok - 1))],
                                     xbuf.at[buf, token_rows(s)], gsem.at[buf])

    def scatter_copy(tile, buf, s):
        return pltpu.make_async_copy(ybuf.at[buf, token_rows(s)], yo_ref.at[token_rows(out_row(tile, s))],
                                     ssem.at[buf])

    def gather(tile, buf):
        for_each_slot(tile, lambda s: gather_copy(tile, buf, s).start())

    def scatter(tile, buf):
        for_each_slot(tile, lambda s: scatter_copy(tile, buf, s).start())

    def wait_rows(tile, whole, one_row):
        @pl.when(nv_ref[tile] == tm)
        def _():
            whole.wait()

        @pl.when(nv_ref[tile] < tm)
        def _():
            def body(s, c):
                one_row.wait()
                return c
            lax.fori_loop(0, nv_ref[tile], body, 0)

    def wait_gather(tile, buf):
        wait_rows(tile, pltpu.make_async_copy(xn_ref.at[pl.ds(0, tm * SUB)], xbuf.at[buf], gsem.at[buf]),
                  pltpu.make_async_copy(xn_ref.at[pl.ds(0, SUB)], xbuf.at[buf, pl.ds(0, SUB)], gsem.at[buf]))

    def wait_scatter(tile, buf):
        wait_rows(tile, pltpu.make_async_copy(ybuf.at[buf], yo_ref.at[pl.ds(0, tm * SUB)], ssem.at[buf]),
                  pltpu.make_async_copy(ybuf.at[buf, pl.ds(0, SUB)], yo_ref.at[pl.ds(0, SUB)], ssem.at[buf]))

    @pl.when(i == 0)
    def _():
        xbuf[...] = jnp.zeros_like(xbuf)

        @pl.when(n_used > 0)
        def _():
            gather(0, 0)

    @pl.when((i >= 2) & (i - 2 < n_used))
    def _():
        wait_scatter(i - 2, slot)

    @pl.when(i < n_used)
    def _():
        wait_gather(i, slot)

        @pl.when(i + 1 < n_used)
        def _():
            gather(i + 1, 1 - slot)

        x = _from_token_major(xbuf.at[slot], tm).astype(BF16)
        gu = jnp.dot(x, wgu_ref[...], preferred_element_type=F32)
        g = gu[:, :ff]
        hh = (g * jax.nn.sigmoid(g)) * gu[:, ff:]
        _to_token_major(ybuf.at[slot], jnp.dot(hh.astype(BF16), wd_ref[...], preferred_element_type=F32))
        scatter(i, slot)

    @pl.when(i == last)
    def _():
        @pl.when((i >= 1) & (i - 1 < n_used))
        def _():
            wait_scatter(i - 1, 1 - slot)

        @pl.when(i < n_used)
        def _():
            wait_scatter(i, slot)


def _experts(tile_expert, n_used, tile_start, tile_valid, sorted_keys, xn_tm, wgu, wd, tm):
    n_tok = xn_tm.shape[0] // SUB
    n_tiles = tile_expert.shape[0]
    _, dm, ff2 = wgu.shape
    assert n_tok & (n_tok - 1) == 0, "token / pick are bit fields of the sorted keys"
    assert dm == SUB * LANES
    w_map = lambda i, te, nu, cs, nv, sk: (te[i], 0, 0)
    return pl.pallas_call(
        functools.partial(_experts_kernel, n_tok=n_tok),
        grid_spec=pltpu.PrefetchScalarGridSpec(
            num_scalar_prefetch=5,
            grid=(n_tiles,),
            in_specs=[pl.BlockSpec(memory_space=pl.ANY),
                      pl.BlockSpec((None, dm, ff2), w_map),
                      pl.BlockSpec((None, ff2 // 2, dm), w_map)],
            out_specs=pl.BlockSpec(memory_space=pl.ANY),
            scratch_shapes=[pltpu.VMEM((2, tm * SUB, LANES), F32), pltpu.VMEM((2, tm * SUB, LANES), F32),
                            pltpu.SemaphoreType.DMA((2,)), pltpu.SemaphoreType.DMA((2,))]),
        out_shape=jax.ShapeDtypeStruct((2 * n_tok * SUB, LANES), F32),
        compiler_params=_params("arbitrary"),
        name="experts",
    )(tile_expert, n_used, tile_start, tile_valid, sorted_keys, xn_tm, wgu, wd)


def _combine_kernel(h2_ref, route_ref, y0_ref, y1_ref, g_ref, o_ref):
    rows = h2_ref.shape[0]
    rec = route_ref[...]
    h = (h2_ref[...] + rec[:, 2:3] * _from_token_major(y0_ref, rows)
         + rec[:, 3:4] * _from_token_major(y1_ref, rows))
    o_ref[...] = _rms(h, g_ref[...])


def _combine(h2, route, yo, g):
    t, dm = h2.shape
    nt = t // TM_PROJ
    tile = lambda w: pl.BlockSpec((TM_PROJ, w), lambda i: (i, 0))
    return pl.pallas_call(
        _combine_kernel,
        grid=(nt,),
        in_specs=[tile(dm), tile(ROUTE_COLS),
                  pl.BlockSpec((TM_PROJ * SUB, LANES), lambda i: (i, 0)),
                  pl.BlockSpec((TM_PROJ * SUB, LANES), lambda i: (i + nt, 0)),
                  pl.BlockSpec((1, dm), lambda i: (0, 0))],
        out_specs=tile(dm),
        out_shape=jax.ShapeDtypeStruct((t, dm), F32),
        compiler_params=_params("arbitrary"),
        name="combine",
    )(h2, route, yo, yo, g)


def kernel(x, mem, norm_mix, w_in, lambda_q1, lambda_k1, lambda_q2, lambda_k2, diff_subln, norm_moba_out, w_out, norm_mem_q, norm_mem_kv, w_mem_q, w_mem_kv, w_mem_o, norm_ffn, w_router_group, b_router_group, w_router_expert, b_router_expert, w_expert_gate, w_expert_up, w_expert_down, norm_final):
    batch, seq, dm = x.shape
    mem_len = mem.shape[1]
    depth = w_in.shape[0]
    t = batch * seq
    assert seq % MOBA_BLOCK == 0 and seq % TQ_CROSS == 0 and t % TM_PROJ == 0

    h = x.reshape(t, dm)
    mem2 = mem.reshape(batch * mem_len, dm)
    row = lambda v: v.reshape(1, -1).astype(F32)
    for l in range(depth):
        lam_init = 0.8 - 0.6 * math.exp(-0.3 * l)
        q_scale = HEAD_DIM ** -0.5 * LOG2E
        qa, ka, va, qd, kd, vd = _inproj(h, row(norm_mix[l]), w_in[l].astype(BF16),
                                         (q_scale, 1.0, 1.0, q_scale, 1.0, 1.0))
        key_aux, val_aux = _key_aux(seq), _value_aux()
        a = _moba(qa, ka, va, key_aux, val_aux, row(norm_moba_out[l]), batch, seq)
        lams = jnp.stack([lambda_q1[l], lambda_k1[l], lambda_q2[l], lambda_k2[l]]).astype(F32)
        d = _diff(qd, kd, vd, key_aux, val_aux, lams, row(diff_subln[l]), batch, seq, lam_init)
        h1, qc = _outproj(a, d, h, w_out[l].astype(BF16), row(norm_mem_q[l]), w_mem_q[l].astype(BF16),
                          (dm // MEM_HEADS) ** -0.5)
        kv = _memkv(mem2, row(norm_mem_kv[l]), w_mem_kv[l].astype(BF16), mem_len)

        pad = ROUTE_COLS - N_GROUPS - N_EXPERTS
        w_r = jnp.pad(jnp.concatenate([w_router_group[l], w_router_expert[l]], axis=1).astype(F32), ((0, 0), (0, pad)))
        w_r_hi = w_r.astype(BF16)
        w_r = jnp.concatenate([w_r_hi, (w_r - w_r_hi.astype(F32)).astype(BF16)], axis=1)
        b_r = jnp.pad(jnp.concatenate([b_router_group[l], b_router_expert[l]]).astype(F32), (0, pad)).reshape(1, -1)
        h2, xn, route, counts = _cross(qc, kv, h1, w_mem_o[l].astype(BF16), row(norm_ffn[l]), w_r, b_r,
                                       batch, seq, mem_len)

        tm = TM_EXPERT
        n_tiles = (2 * t) // tm + N_EXPERTS
        cnt = counts[0, EXPERT_COL0:EXPERT_COL0 + N_EXPERTS].astype(jnp.int32)
        tiles_of = (cnt + tm - 1) // tm
        tile_end = jnp.cumsum(tiles_of)
        tile_idx = jnp.arange(n_tiles, dtype=jnp.int32)
        tile_expert = jnp.minimum(jnp.sum((tile_end[None, :] <= tile_idx[:, None]).astype(jnp.int32), axis=1),
                                  N_EXPERTS - 1)
        n_used = tile_end[-1:].astype(jnp.int32)
        onehot_te = (tile_expert[:, None] == jnp.arange(N_EXPERTS, dtype=jnp.int32)[None, :]).astype(jnp.int32)
        pick = lambda table: jnp.sum(onehot_te * table[None, :], axis=1)
        in_expert = (tile_idx - pick(tile_end - tiles_of)) * tm
        tile_start = pick(jnp.cumsum(cnt) - cnt) + in_expert
        tile_valid = jnp.where(tile_idx < n_used[0], jnp.clip(pick(cnt) - in_expert, 0, tm), 0).astype(jnp.int32)
        out_row = jnp.arange(t, dtype=jnp.int32)[:, None] + jnp.array([0, t], jnp.int32)[None, :]
        keys = route[:, 0:2].astype(jnp.int32) * (2 * t) + out_row
        sorted_keys = jnp.sort(keys.reshape(-1))

        wgu = jnp.concatenate([w_expert_gate[l], w_expert_up[l]], axis=2).astype(BF16)
        yo = _experts(tile_expert, n_used, tile_start.astype(jnp.int32), tile_valid, sorted_keys, xn, wgu,
                      w_expert_down[l].astype(BF16), tm)
        assert depth == 1
        h = _combine(h2, route, yo, row(norm_final))
    return h.reshape(batch, seq, dm)
```

```python
import functools
import math

import jax
import jax.numpy as jnp
import numpy as np
from jax import lax
from jax.experimental import pallas as pl
from jax.experimental.pallas import tpu as pltpu

F32 = jnp.float32
BF16 = jnp.bfloat16

HEAD_DIM = 64
MOBA_HEADS = 8
MOBA_WIDTH = MOBA_HEADS * HEAD_DIM
MOBA_BLOCK = 256
MOBA_TOPK = 3
DIFF_HEADS = 4
DIFF_V_DIM = 2 * HEAD_DIM
DIFF_WIDTH = DIFF_HEADS * DIFF_V_DIM
MEM_HEADS = 4
N_GROUPS = 4
EXPERTS_PER_GROUP = 8
N_EXPERTS = N_GROUPS * EXPERTS_PER_GROUP
RMS_EPS = 1e-6
NEG_INF = -1e30
LOG2E = math.log2(math.e)
LANES = 128
ROUTE_COLS = LANES
EXPERT_COL0 = N_GROUPS

TM_PROJ = 512
TQ_ATTN = MOBA_BLOCK
TQ_CROSS = 512
TM_EXPERT = 256
VMEM_LIMIT = 56 * 1024 * 1024


def _alibi_slopes(n):
    return [2.0 ** (-8.0 * (i + 1) / n) for i in range(n)]


def _rms(x, g):
    y = x * lax.rsqrt(jnp.mean(x * x, axis=-1, keepdims=True) + RMS_EPS)
    return y * g


def _dot_nt(a, b):
    return lax.dot_general(a, b, (((1,), (1,)), ((), ())), preferred_element_type=F32)


def _params(*sem):
    return pltpu.CompilerParams(dimension_semantics=sem, vmem_limit_bytes=VMEM_LIMIT)


def _inproj_kernel(x_ref, g_ref, w_ref, *out_refs, scales):
    xn = _rms(x_ref[...], g_ref[...]).astype(BF16)
    width = out_refs[0].shape[1]
    for i, o_ref in enumerate(out_refs):
        y = jnp.dot(xn, w_ref[:, i * width:(i + 1) * width], preferred_element_type=F32)
        o_ref[...] = (y if scales[i] == 1.0 else y * scales[i]).astype(BF16)


def _inproj(x2, g, w, scales):
    t, d = x2.shape
    n_out = w.shape[1] // MOBA_WIDTH
    return pl.pallas_call(
        functools.partial(_inproj_kernel, scales=scales),
        grid=(t // TM_PROJ,),
        in_specs=[pl.BlockSpec((TM_PROJ, d), lambda i: (i, 0)),
                  pl.BlockSpec((1, d), lambda i: (0, 0)),
                  pl.BlockSpec(w.shape, lambda i: (0, 0))],
        out_specs=[pl.BlockSpec((TM_PROJ, MOBA_WIDTH), lambda i: (i, 0))] * n_out,
        out_shape=[jax.ShapeDtypeStruct((t, MOBA_WIDTH), BF16)] * n_out,
        compiler_params=_params("arbitrary"),
        name="inproj",
    )(x2, g, w)


AUX_PEN0 = 0
AUX_BLK0 = 64
AUX_OFF0 = 68
AUX_ROW = 72
N_SPLIT = 4


def _bf16_pieces(c):
    pieces, rem = [], np.float64(c)
    for _ in range(N_SPLIT):
        p = np.float64(np.asarray(rem).astype(BF16))
        pieces.append(p)
        rem = rem - p
    return pieces


def _key_aux(seq):
    pos = np.arange(seq)
    blk, off = pos // MOBA_BLOCK, pos % MOBA_BLOCK
    assert blk.max() < 8, "penalty lanes hold 8 blocks per head"
    a = np.zeros((seq, LANES), np.float32)
    for h in range(MOBA_HEADS):
        a[pos, AUX_PEN0 + h * 8 + blk] = 1.0
    a[:, AUX_BLK0:AUX_BLK0 + N_SPLIT] = (blk * MOBA_BLOCK)[:, None]
    a[:, AUX_OFF0:AUX_OFF0 + N_SPLIT] = off[:, None]
    a[:, AUX_ROW] = 1.0
    return jnp.asarray(a, BF16)


def _value_aux():
    a = np.zeros((2 * MOBA_BLOCK, LANES), np.float32)
    a[:, 0] = 1.0
    return jnp.asarray(a, BF16)


def _query_aux_rows(slopes):
    a = np.zeros((len(slopes), LANES), np.float32)
    for i, s in enumerate(slopes):
        a[i, AUX_BLK0:AUX_BLK0 + N_SPLIT] = _bf16_pieces(s * LOG2E)
        a[i, AUX_OFF0:AUX_OFF0 + N_SPLIT] = _bf16_pieces(s * LOG2E)
    return jnp.asarray(a)


def _query_aux(crow, slope, j, pen_t=None, pen_lo=0):
    blk = MOBA_BLOCK
    lane = lax.broadcasted_iota(jnp.int32, (blk, LANES), 1)
    t_q = (j * blk).astype(F32) + lax.broadcasted_iota(jnp.int32, (blk, LANES), 0).astype(F32)
    aux = jnp.where(lane == AUX_ROW, (-slope * LOG2E) * t_q, crow)
    if pen_t is not None:
        aux = jnp.where((lane >= pen_lo) & (lane < pen_lo + 8), pen_t, aux)
    return aux.astype(BF16)


def _attend(qaug_ref, k_ref, v_ref, kaux_ref, vaux_ref, s_ref, sown_ref, mrun_ref, acc_ref, j, kcols, vcols):
    n_items = qaug_ref.shape[0]
    blk = MOBA_BLOCK
    causal = (lax.broadcasted_iota(jnp.int32, (blk, blk), 0) >= lax.broadcasted_iota(jnp.int32, (blk, blk), 1))
    own = pl.ds(pl.multiple_of(j * blk, blk), blk)

    def scores(i, rows, kaux):
        return _dot_nt(qaug_ref[i], jnp.concatenate([k_ref[rows, kcols[i]], kaux], axis=1))

    def half_max(s):
        return jnp.maximum(s[:, :LANES], s[:, LANES:])

    kaux_own = kaux_ref[own, :]
    for i in range(n_items):
        s = jnp.where(causal, scores(i, own, kaux_own), NEG_INF)
        sown_ref[i] = s
        mrun_ref[i] = half_max(s)

    def pass_a(n, width):
        rows = pl.ds(pl.multiple_of(n * blk, blk), width * blk)
        kaux_n = kaux_ref[rows, :]
        for i in range(n_items):
            s = scores(i, rows, kaux_n)
            m = mrun_ref[i]
            for w in range(width):
                sw = s[:, w * blk:(w + 1) * blk]
                s_ref[i, n + w] = sw
                m = jnp.maximum(m, half_max(sw))
            mrun_ref[i] = m

    def loop_past(fn):
        def pair(t, c):
            fn(2 * t, 2)
            return c
        lax.fori_loop(0, j // 2, pair, 0)

        @pl.when(j % 2 == 1)
        def _():
            fn(j - 1, 1)

    loop_past(pass_a)

    for i in range(n_items):
        mrun_ref[i] = jnp.broadcast_to(jnp.max(mrun_ref[i], axis=1, keepdims=True), (blk, LANES))

    def probs(i, s):
        m = mrun_ref[i]
        return jnp.concatenate([jnp.exp2(s[:, :LANES] - m), jnp.exp2(s[:, LANES:] - m)], axis=1).astype(BF16)

    def pv(i, p, rows, width):
        v_aug = jnp.concatenate([v_ref[rows, vcols[i]], vaux_ref[:width * blk, :]], axis=1)
        return jnp.dot(p, v_aug, preferred_element_type=F32)

    for i in range(n_items):
        acc_ref[i] = pv(i, probs(i, sown_ref[i]), own, 1)

    def pass_b(n, width):
        rows = pl.ds(pl.multiple_of(n * blk, blk), width * blk)
        for i in range(n_items):
            p = jnp.concatenate([probs(i, s_ref[i, n + w]) for w in range(width)], axis=1)
            acc_ref[i] += pv(i, p, rows, width)

    loop_past(pass_b)

    outs = []
    for i in range(n_items):
        a = acc_ref[i]
        outs.append(a[:, :LANES] * (1.0 / a[:, LANES:LANES + 1]))
    return outs


def _attend_scratch(n_items, nblk):
    blk = MOBA_BLOCK
    return [pltpu.VMEM((n_items, blk, 2 * LANES), BF16),
            pltpu.VMEM((n_items, nblk, blk, blk), F32),
            pltpu.VMEM((n_items, blk, blk), F32),
            pltpu.VMEM((n_items, blk, LANES), F32),
            pltpu.VMEM((n_items, blk, 2 * LANES), F32)]


def _moba_kernel(q_ref, k_ref, v_ref, kaux_ref, vaux_ref, crow_ref, g_ref, o_ref,
                 kmean_ref, kbd_ref, pen_ref, qaug_ref, s_ref, sown_ref, mrun_ref, acc_ref):
    j = pl.program_id(1)
    blk = MOBA_BLOCK
    nblk = s_ref.shape[1]
    slopes = _alibi_slopes(MOBA_HEADS)
    n_slabs = MOBA_WIDTH // LANES

    @pl.when(j == 0)
    def _():
        kmean_ref[...] = jnp.zeros_like(kmean_ref)
        for n in range(nblk):
            kb = k_ref[n * blk:(n + 1) * blk, :].astype(F32)
            kmean_ref[n:n + 1, :] = jnp.sum(kb, axis=0, keepdims=True) * (1.0 / blk)
        col_head = lax.broadcasted_iota(jnp.int32, kmean_ref.shape, 1) // HEAD_DIM
        for h in range(MOBA_HEADS):
            kbd_ref[h * 8:(h + 1) * 8, :] = jnp.where(col_head == h, kmean_ref[...], 0.0)

    pen_ref[...] = jnp.zeros_like(pen_ref)

    @pl.when(j > MOBA_TOPK)
    def _():
        kbd = kbd_ref[...]
        kbd_hi = kbd.astype(BF16)
        kbd_lo = (kbd - kbd_hi.astype(F32)).astype(BF16)
        gate = _dot_nt(kbd_hi, q_ref[...]) + _dot_nt(kbd_lo, q_ref[...])
        n_iota = lax.broadcasted_iota(jnp.int32, (8, blk), 0)
        for h in range(MOBA_HEADS):
            g = gate[h * 8:(h + 1) * 8, :]
            rank = jnp.zeros((8, blk), jnp.int32)
            for m in range(nblk):
                gm = g[m:m + 1, :]
                ahead = (gm > g) | ((gm == g) & (m < n_iota))
                rank = rank + jnp.where(ahead, 1, 0) * (m < j).astype(jnp.int32)
            pen_ref[h * 8:(h + 1) * 8, :] = jnp.where((n_iota < j) & (rank >= MOBA_TOPK), NEG_INF, 0.0)

    pen_t = pen_ref[...].T

    lane = lax.broadcasted_iota(jnp.int32, (blk, LANES), 1)
    for h in range(MOBA_HEADS):
        half = h % 2
        q_slab = q_ref[:, (h // 2) * LANES:(h // 2 + 1) * LANES]
        in_head = (lane >= half * HEAD_DIM) & (lane < (half + 1) * HEAD_DIM)
        aux = _query_aux(crow_ref[h:h + 1, :], slopes[h], j, pen_t, AUX_PEN0 + h * 8)
        qaug_ref[h] = jnp.concatenate([jnp.where(in_head, q_slab, 0).astype(BF16), aux], axis=1)

    cols = [slice((h // 2) * LANES, (h // 2 + 1) * LANES) for h in range(MOBA_HEADS)]
    outs = _attend(qaug_ref, k_ref, v_ref, kaux_ref, vaux_ref, s_ref, sown_ref, mrun_ref, acc_ref, j, cols, cols)
    o = jnp.concatenate([jnp.where(lane < HEAD_DIM, outs[2 * p], outs[2 * p + 1]) for p in range(n_slabs)], axis=1)
    o_ref[...] = _rms(o, g_ref[...]).astype(BF16)


def _moba(qa, ka, va, key_aux, val_aux, g, batch, seq):
    nblk = seq // MOBA_BLOCK
    w = MOBA_WIDTH
    crow = _query_aux_rows(_alibi_slopes(MOBA_HEADS))
    whole = lambda arr: pl.BlockSpec(arr.shape, lambda b, j: (0, 0))
    return pl.pallas_call(
        _moba_kernel,
        grid=(batch, nblk),
        in_specs=[pl.BlockSpec((MOBA_BLOCK, w), lambda b, j: (b * nblk + j, 0)),
                  pl.BlockSpec((seq, w), lambda b, j: (b, 0)),
                  pl.BlockSpec((seq, w), lambda b, j: (b, 0)),
                  whole(key_aux), whole(val_aux), whole(crow), whole(g)],
        out_specs=pl.BlockSpec((MOBA_BLOCK, w), lambda b, j: (b * nblk + j, 0)),
        out_shape=jax.ShapeDtypeStruct((batch * seq, w), BF16),
        scratch_shapes=[pltpu.VMEM((8, w), F32), pltpu.VMEM((8 * MOBA_HEADS, w), F32),
                        pltpu.VMEM((LANES, MOBA_BLOCK), F32)]
        + _attend_scratch(MOBA_HEADS, nblk),
        compiler_params=_params("arbitrary", "arbitrary"),
        name="moba",
    )(qa, ka, va, key_aux, val_aux, crow, g)


def _diff_kernel(q_ref, k_ref, v_ref, kaux_ref, vaux_ref, crow_ref, lam_ref, g_ref, o_ref,
                 qaug_ref, s_ref, sown_ref, mrun_ref, acc_ref, *, lam_init):
    j = pl.program_id(1)
    blk = MOBA_BLOCK
    slopes = _alibi_slopes(DIFF_HEADS)
    lane = lax.broadcasted_iota(jnp.int32, (blk, LANES), 1)

    lv = lam_ref[...]
    lam = (jnp.exp(jnp.sum(lv[0:1] * lv[1:2], axis=1, keepdims=True))
           - jnp.exp(jnp.sum(lv[2:3] * lv[3:4], axis=1, keepdims=True)) + lam_init)

    for i in range(2 * DIFF_HEADS):
        h, c = i // 2, i % 2
        q_slab = q_ref[:, h * LANES:(h + 1) * LANES]
        in_map = (lane >= c * HEAD_DIM) & (lane < (c + 1) * HEAD_DIM)
        aux = _query_aux(crow_ref[i:i + 1, :], slopes[h], j)
        qaug_ref[i] = jnp.concatenate([jnp.where(in_map, q_slab, 0).astype(BF16), aux], axis=1)

    cols = [slice((i // 2) * LANES, (i // 2 + 1) * LANES) for i in range(2 * DIFF_HEADS)]
    outs = _attend(qaug_ref, k_ref, v_ref, kaux_ref, vaux_ref, s_ref, sown_ref, mrun_ref, acc_ref, j, cols, cols)
    for h in range(DIFF_HEADS):
        o = outs[2 * h] - lam * outs[2 * h + 1]
        o_ref[:, h * LANES:(h + 1) * LANES] = (_rms(o, g_ref[...]) * (1.0 - lam_init)).astype(BF16)


def _diff(qd, kd, vd, key_aux, val_aux, lams, g, batch, seq, lam_init):
    nblk = seq // MOBA_BLOCK
    w = DIFF_WIDTH
    crow = _query_aux_rows([s for s in _alibi_slopes(DIFF_HEADS) for _ in range(2)])
    whole = lambda arr: pl.BlockSpec(arr.shape, lambda b, j: (0, 0))
    return pl.pallas_call(
        functools.partial(_diff_kernel, lam_init=lam_init),
        grid=(batch, nblk),
        in_specs=[pl.BlockSpec((MOBA_BLOCK, w), lambda b, j: (b * nblk + j, 0)),
                  pl.BlockSpec((seq, w), lambda b, j: (b, 0)),
                  pl.BlockSpec((seq, w), lambda b, j: (b, 0)),
                  whole(key_aux), whole(val_aux), whole(crow), whole(lams), whole(g)],
        out_specs=pl.BlockSpec((MOBA_BLOCK, w), lambda b, j: (b * nblk + j, 0)),
        out_shape=jax.ShapeDtypeStruct((batch * seq, w), BF16),
        scratch_shapes=_attend_scratch(2 * DIFF_HEADS, nblk),
        compiler_params=_params("arbitrary", "arbitrary"),
        name="diff",
    )(qd, kd, vd, key_aux, val_aux, crow, lams, g)


def _outproj_kernel(a_ref, d_ref, x_ref, wo_ref, g_ref, wq_ref, h_ref, q_ref, *, q_scale):
    wa = a_ref.shape[1]
    h = (x_ref[...]
         + jnp.dot(a_ref[...], wo_ref[:wa, :], preferred_element_type=F32)
         + jnp.dot(d_ref[...], wo_ref[wa:, :], preferred_element_type=F32))
    h_ref[...] = h
    qn = _rms(h, g_ref[...]).astype(BF16)
    q_ref[...] = (jnp.dot(qn, wq_ref[...], preferred_element_type=F32) * q_scale).astype(BF16)


def _outproj(a, d, x2, w_out, g, w_q, q_scale):
    t, dm = x2.shape
    tile = lambda w: pl.BlockSpec((TM_PROJ, w), lambda i: (i, 0))
    whole = lambda arr: pl.BlockSpec(arr.shape, lambda i: (0, 0))
    return pl.pallas_call(
        functools.partial(_outproj_kernel, q_scale=q_scale),
        grid=(t // TM_PROJ,),
        in_specs=[tile(a.shape[1]), tile(d.shape[1]), tile(dm), whole(w_out), whole(g), whole(w_q)],
        out_specs=[tile(dm), tile(dm)],
        out_shape=[jax.ShapeDtypeStruct((t, dm), F32), jax.ShapeDtypeStruct((t, dm), BF16)],
        compiler_params=_params("arbitrary"),
        name="outproj",
    )(a, d, x2, w_out, g, w_q)


def _memkv_kernel(m_ref, g_ref, w_ref, kv_ref):
    mn = _rms(m_ref[...], g_ref[...]).astype(BF16)
    kv_ref[...] = jnp.dot(mn, w_ref[...], preferred_element_type=F32).astype(BF16)


def _memkv(mem2, g, w_kv, mem_len):
    rows, dm = mem2.shape
    return pl.pallas_call(
        _memkv_kernel,
        grid=(rows // mem_len,),
        in_specs=[pl.BlockSpec((mem_len, dm), lambda i: (i, 0)),
                  pl.BlockSpec((1, dm), lambda i: (0, 0)),
                  pl.BlockSpec(w_kv.shape, lambda i: (0, 0))],
        out_specs=pl.BlockSpec((mem_len, w_kv.shape[1]), lambda i: (i, 0)),
        out_shape=jax.ShapeDtypeStruct((rows, w_kv.shape[1]), BF16),
        compiler_params=_params("arbitrary"),
        name="memkv",
    )(mem2, g, w_kv)


def _cross_kernel(q_ref, kv_ref, h_ref, wo_ref, g_ref, wr_ref, br_ref,
                  h2_ref, xn_ref, route_ref, counts_ref, run_ref):
    first = (pl.program_id(0) == 0) & (pl.program_id(1) == 0)

    @pl.when(first)
    def _():
        run_ref[...] = jnp.zeros_like(run_ref)

    tq, dm = h_ref.shape
    dh = dm // MEM_HEADS
    heads = []
    for h in range(MEM_HEADS):
        s = _dot_nt(q_ref[:, h * dh:(h + 1) * dh], kv_ref[:, h * dh:(h + 1) * dh])
        m = jnp.max(s, axis=-1, keepdims=True)
        e = jnp.exp(s - m)
        l = jnp.sum(e, axis=-1, keepdims=True)
        o = jnp.dot(e.astype(BF16), kv_ref[:, dm + h * dh:dm + (h + 1) * dh], preferred_element_type=F32)
        heads.append((o * (1.0 / l)).astype(BF16))
    o = jnp.concatenate(heads, axis=1)
    h2 = h_ref[...] + jnp.dot(o, wo_ref[...], preferred_element_type=F32)
    h2_ref[...] = h2
    xn = _rms(h2, g_ref[...])
    _to_token_major(xn_ref, xn)

    x_hi = xn.astype(BF16)
    x_lo = (xn - x_hi.astype(F32)).astype(BF16)
    hi = jnp.dot(x_hi, wr_ref[...], preferred_element_type=F32)
    lo_hi = jnp.dot(x_lo, wr_ref[:, :ROUTE_COLS], preferred_element_type=F32)
    logits = hi[:, :ROUTE_COLS] + (hi[:, ROUTE_COLS:] + lo_hi) + br_ref[...]
    lane = lax.broadcasted_iota(jnp.int32, logits.shape, 1).astype(F32)
    big = float(ROUTE_COLS)

    def top1(vals):
        v = jnp.max(vals, axis=-1, keepdims=True)
        i = jnp.min(jnp.where(vals == v, lane, big), axis=-1, keepdims=True)
        return v, i

    gl = jnp.where(lane < N_GROUPS, logits, -jnp.inf)
    g_max, g_idx = top1(gl)
    g_w = 1.0 / jnp.sum(jnp.exp(gl - g_max), axis=-1, keepdims=True)
    lo = EXPERT_COL0 + g_idx * EXPERTS_PER_GROUP
    el = jnp.where((lane >= lo) & (lane < lo + EXPERTS_PER_GROUP), logits, -jnp.inf)
    v1, i1 = top1(el)
    v2, i2 = top1(jnp.where(lane == i1, -jnp.inf, el))
    e2 = jnp.exp(v2 - v1)
    w1 = g_w * (1.0 / (1.0 + e2))
    w2 = g_w * (e2 / (1.0 + e2))

    onehot = jnp.where((lane == i1) | (lane == i2), 1.0, 0.0)
    run_ref[...] = run_ref[...] + jnp.sum(onehot, axis=0, keepdims=True)
    counts_ref[...] = run_ref[...]

    rec = jnp.zeros(logits.shape, F32)
    for col, val in enumerate((i1 - EXPERT_COL0, i2 - EXPERT_COL0, w1, w2)):
        rec = jnp.where(lane == col, val, rec)
    route_ref[...] = rec


def _cross(qc, kv, h1, w_o, g, w_r, b_r, batch, seq, mem_len):
    t, dm = h1.shape
    nt = seq // TQ_CROSS
    tile = lambda w: pl.BlockSpec((TQ_CROSS, w), lambda b, i: (b * nt + i, 0))
    whole = lambda arr: pl.BlockSpec(arr.shape, lambda b, i: (0, 0))
    return pl.pallas_call(
        _cross_kernel,
        grid=(batch, nt),
        in_specs=[tile(dm), pl.BlockSpec((mem_len, kv.shape[1]), lambda b, i: (b, 0)), tile(dm),
                  whole(w_o), whole(g), whole(w_r), whole(b_r)],
        out_specs=[tile(dm), pl.BlockSpec((TQ_CROSS * SUB, LANES), lambda b, i: (b * nt + i, 0)),
                   tile(ROUTE_COLS), pl.BlockSpec((1, ROUTE_COLS), lambda b, i: (0, 0))],
        out_shape=[jax.ShapeDtypeStruct((t, dm), F32), jax.ShapeDtypeStruct((t * SUB, LANES), F32),
                   jax.ShapeDtypeStruct((t, ROUTE_COLS), F32), jax.ShapeDtypeStruct((1, ROUTE_COLS), F32)],
        scratch_shapes=[pltpu.VMEM((1, ROUTE_COLS), F32)],
        compiler_params=_params("arbitrary", "arbitrary"),
        name="cross",
    )(qc, kv, h1, w_o, g, w_r, b_r)


SUB = 8


def _to_token_major(ref, x):
    rows = x.shape[0]
    for c in range(x.shape[1] // LANES):
        ref[pl.ds(c, rows, stride=SUB), :] = x[:, c * LANES:(c + 1) * LANES]


def _from_token_major(ref, rows):
    n_chunks = ref.shape[0] // rows
    return jnp.concatenate([ref[pl.ds(c, rows, stride=SUB), :] for c in range(n_chunks)], axis=1)


def _experts_kernel(te_ref, nused_ref, cs_ref, nv_ref, rows_ref, xn_ref, wg_ref, wu_ref, wd_ref, yo_ref,
                    xbuf, ybuf, wgu_bf, wd_bf, gsem, ssem, *, n_tok):
    i = pl.program_id(0)
    last = pl.num_programs(0) - 1
    n_used = nused_ref[0]
    tm = xbuf.shape[1] // SUB
    ff = wd_ref.shape[0]
    slot = i % 2
    token_rows = lambda tok: pl.ds(pl.multiple_of(tok * SUB, SUB), SUB)

    def for_each_slot(tile, fn):
        base = cs_ref[tile]

        @pl.when(nv_ref[tile] == tm)
        def _():
            for s in range(tm):
                fn(rows_ref[base + s], s)

        @pl.when(nv_ref[tile] < tm)
        def _():
            def body(s, c):
                fn(rows_ref[base + s], s)
                return c
            lax.fori_loop(0, nv_ref[tile], body, 0)

    def gather(tile, buf):
        for_each_slot(tile, lambda row, s: pltpu.make_async_copy(
            xn_ref.at[token_rows(row & (n_tok - 1))], xbuf.at[buf, token_rows(s)], gsem.at[buf]).start())

    def scatter(tile, buf):
        for_each_slot(tile, lambda row, s: pltpu.make_async_copy(
            ybuf.at[buf, token_rows(s)], yo_ref.at[token_rows(row)], ssem.at[buf]).start())

    @pl.when((i < n_used) & ((i == 0) | (te_ref[i] != te_ref[jnp.maximum(i - 1, 0)])))
    def _():
        wgu_bf[:, :ff] = wg_ref[...].astype(BF16)
        wgu_bf[:, ff:] = wu_ref[...].astype(BF16)
        wd_bf[...] = wd_ref[...].astype(BF16)

    def wait_rows(tile, whole, one_row):
        @pl.when(nv_ref[tile] == tm)
        def _():
            whole.wait()

        @pl.when(nv_ref[tile] < tm)
        def _():
            def body(s, c):
                one_row.wait()
                return c
            lax.fori_loop(0, nv_ref[tile], body, 0)

    def wait_gather(tile, buf):
        wait_rows(tile, pltpu.make_async_copy(xn_ref.at[pl.ds(0, tm * SUB)], xbuf.at[buf], gsem.at[buf]),
                  pltpu.make_async_copy(xn_ref.at[pl.ds(0, SUB)], xbuf.at[buf, pl.ds(0, SUB)], gsem.at[buf]))

    def wait_scatter(tile, buf):
        wait_rows(tile, pltpu.make_async_copy(ybuf.at[buf], yo_ref.at[pl.ds(0, tm * SUB)], ssem.at[buf]),
                  pltpu.make_async_copy(ybuf.at[buf, pl.ds(0, SUB)], yo_ref.at[pl.ds(0, SUB)], ssem.at[buf]))

    @pl.when(i == 0)
    def _():
        xbuf[...] = jnp.zeros_like(xbuf)

        @pl.when(n_used > 0)
        def _():
            gather(0, 0)

    @pl.when((i >= 2) & (i - 2 < n_used))
    def _():
        wait_scatter(i - 2, slot)

    @pl.when(i < n_used)
    def _():
        wait_gather(i, slot)

        @pl.when(i + 1 < n_used)
        def _():
            gather(i + 1, 1 - slot)

        x = _from_token_major(xbuf.at[slot], tm).astype(BF16)
        gu = jnp.dot(x, wgu_bf[...], preferred_element_type=F32)
        g = gu[:, :ff]
        hh = (g * jax.nn.sigmoid(g)) * gu[:, ff:]
        _to_token_major(ybuf.at[slot], jnp.dot(hh.astype(BF16), wd_bf[...], preferred_element_type=F32))
        scatter(i, slot)

    @pl.when(i == last)
    def _():
        @pl.when((i >= 1) & (i - 1 < n_used))
        def _():
            wait_scatter(i - 1, 1 - slot)

        @pl.when(i < n_used)
        def _():
            wait_scatter(i, slot)


def _experts(tile_expert, n_used, tile_start, tile_valid, sorted_rows, xn_tm, w_gate, w_up, w_down, tm):
    n_tok = xn_tm.shape[0] // SUB
    n_tiles = tile_expert.shape[0]
    _, dm, ff = w_gate.shape
    assert n_tok & (n_tok - 1) == 0, "source token = output row & (n_tok - 1)"
    assert dm == SUB * LANES
    w_map = lambda i, te, nu, cs, nv, sr: (te[i], 0, 0)
    return pl.pallas_call(
        functools.partial(_experts_kernel, n_tok=n_tok),
        grid_spec=pltpu.PrefetchScalarGridSpec(
            num_scalar_prefetch=5,
            grid=(n_tiles,),
            in_specs=[pl.BlockSpec(memory_space=pl.ANY),
                      pl.BlockSpec((None, dm, ff), w_map), pl.BlockSpec((None, dm, ff), w_map),
                      pl.BlockSpec((None, ff, dm), w_map)],
            out_specs=pl.BlockSpec(memory_space=pl.ANY),
            scratch_shapes=[pltpu.VMEM((2, tm * SUB, LANES), F32), pltpu.VMEM((2, tm * SUB, LANES), F32),
                            pltpu.VMEM((dm, 2 * ff), BF16), pltpu.VMEM((ff, dm), BF16),
                            pltpu.SemaphoreType.DMA((2,)), pltpu.SemaphoreType.DMA((2,))]),
        out_shape=jax.ShapeDtypeStruct((2 * n_tok * SUB, LANES), F32),
        compiler_params=_params("arbitrary"),
        name="experts",
    )(tile_expert, n_used, tile_start, tile_valid, sorted_rows, xn_tm, w_gate, w_up, w_down)


def _combine_kernel(h2_ref, route_ref, y0_ref, y1_ref, g_ref, o_ref):
    rows = h2_ref.shape[0]
    rec = route_ref[...]
    h = (h2_ref[...] + rec[:, 2:3] * _from_token_major(y0_ref, rows)
         + rec[:, 3:4] * _from_token_major(y1_ref, rows))
    o_ref[...] = _rms(h, g_ref[...])


def _combine(h2, route, yo, g):
    t, dm = h2.shape
    nt = t // TM_PROJ
    tile = lambda w: pl.BlockSpec((TM_PROJ, w), lambda i: (i, 0))
    return pl.pallas_call(
        _combine_kernel,
        grid=(nt,),
        in_specs=[tile(dm), tile(ROUTE_COLS),
                  pl.BlockSpec((TM_PROJ * SUB, LANES), lambda i: (i, 0)),
                  pl.BlockSpec((TM_PROJ * SUB, LANES), lambda i: (i + nt, 0)),
                  pl.BlockSpec((1, dm), lambda i: (0, 0))],
        out_specs=tile(dm),
        out_shape=jax.ShapeDtypeStruct((t, dm), F32),
        compiler_params=_params("arbitrary"),
        name="combine",
    )(h2, route, yo, yo, g)


def kernel(x, mem, norm_mix, w_in, lambda_q1, lambda_k1, lambda_q2, lambda_k2, diff_subln, norm_moba_out, w_out, norm_mem_q, norm_mem_kv, w_mem_q, w_mem_kv, w_mem_o, norm_ffn, w_router_group, b_router_group, w_router_expert, b_router_expert, w_expert_gate, w_expert_up, w_expert_down, norm_final):
    batch, seq, dm = x.shape
    mem_len = mem.shape[1]
    depth = w_in.shape[0]
    t = batch * seq
    assert seq % MOBA_BLOCK == 0 and seq % TQ_CROSS == 0 and t % TM_PROJ == 0

    h = x.reshape(t, dm)
    mem2 = mem.reshape(batch * mem_len, dm)
    row = lambda v: v.reshape(1, -1).astype(F32)
    for l in range(depth):
        lam_init = 0.8 - 0.6 * math.exp(-0.3 * l)
        q_scale = HEAD_DIM ** -0.5 * LOG2E
        qa, ka, va, qd, kd, vd = _inproj(h, row(norm_mix[l]), w_in[l].astype(BF16),
                                         (q_scale, 1.0, 1.0, q_scale, 1.0, 1.0))
        key_aux, val_aux = _key_aux(seq), _value_aux()
        a = _moba(qa, ka, va, key_aux, val_aux, row(norm_moba_out[l]), batch, seq)
        lams = jnp.stack([lambda_q1[l], lambda_k1[l], lambda_q2[l], lambda_k2[l]]).astype(F32)
        d = _diff(qd, kd, vd, key_aux, val_aux, lams, row(diff_subln[l]), batch, seq, lam_init)
        h1, qc = _outproj(a, d, h, w_out[l].astype(BF16), row(norm_mem_q[l]), w_mem_q[l].astype(BF16),
                          (dm // MEM_HEADS) ** -0.5)
        kv = _memkv(mem2, row(norm_mem_kv[l]), w_mem_kv[l].astype(BF16), mem_len)

        pad = ROUTE_COLS - N_GROUPS - N_EXPERTS
        w_r = jnp.pad(jnp.concatenate([w_router_group[l], w_router_expert[l]], axis=1).astype(F32), ((0, 0), (0, pad)))
        w_r_hi = w_r.astype(BF16)
        w_r = jnp.concatenate([w_r_hi, (w_r - w_r_hi.astype(F32)).astype(BF16)], axis=1)
        b_r = jnp.pad(jnp.concatenate([b_router_group[l], b_router_expert[l]]).astype(F32), (0, pad)).reshape(1, -1)
        h2, xn, route, counts = _cross(qc, kv, h1, w_mem_o[l].astype(BF16), row(norm_ffn[l]), w_r, b_r,
                                       batch, seq, mem_len)

        tm = TM_EXPERT
        n_tiles = (2 * t) // tm + N_EXPERTS
        cnt = counts[0, EXPERT_COL0:EXPERT_COL0 + N_EXPERTS].astype(jnp.int32)
        tiles_of = (cnt + tm - 1) // tm
        tile_end = jnp.cumsum(tiles_of)
        tile_idx = jnp.arange(n_tiles, dtype=jnp.int32)
        tile_expert = jnp.minimum(jnp.sum((tile_end[None, :] <= tile_idx[:, None]).astype(jnp.int32), axis=1),
                                  N_EXPERTS - 1)
        n_used = tile_end[-1:].astype(jnp.int32)
        onehot_te = (tile_expert[:, None] == jnp.arange(N_EXPERTS, dtype=jnp.int32)[None, :]).astype(jnp.int32)
        pick = lambda table: jnp.sum(onehot_te * table[None, :], axis=1)
        in_expert = (tile_idx - pick(tile_end - tiles_of)) * tm
        tile_start = pick(jnp.cumsum(cnt) - cnt) + in_expert
        tile_valid = jnp.where(tile_idx < n_used[0], jnp.clip(pick(cnt) - in_expert, 0, tm), 0).astype(jnp.int32)
        out_row = jnp.arange(t, dtype=jnp.int32)[:, None] + jnp.array([0, t], jnp.int32)[None, :]
        keys = route[:, 0:2].astype(jnp.int32) * (2 * t) + out_row
        sorted_rows = jnp.sort(keys.reshape(-1)) & (2 * t - 1)

        yo = _experts(tile_expert, n_used, tile_start.astype(jnp.int32), tile_valid, sorted_rows, xn,
                      w_expert_gate[l].astype(F32), w_expert_up[l].astype(F32), w_expert_down[l].astype(F32), tm)
        assert depth == 1
        h = _combine(h2, route, yo, row(norm_final))
    return h.reshape(batch, seq, dm)
```

```python
import functools
import math

import jax
import jax.numpy as jnp
import numpy as np
from jax import lax
from jax.experimental import pallas as pl
from jax.experimental.pallas import tpu as pltpu

F32 = jnp.float32
BF16 = jnp.bfloat16

HEAD_DIM = 64
MOBA_HEADS = 8
MOBA_WIDTH = MOBA_HEADS * HEAD_DIM
MOBA_BLOCK = 256
MOBA_TOPK = 3
DIFF_HEADS = 4
DIFF_V_DIM = 2 * HEAD_DIM
DIFF_WIDTH = DIFF_HEADS * DIFF_V_DIM
MEM_HEADS = 4
N_GROUPS = 4
EXPERTS_PER_GROUP = 8
N_EXPERTS = N_GROUPS * EXPERTS_PER_GROUP
RMS_EPS = 1e-6
NEG_INF = -1e30
LOG2E = math.log2(math.e)
LANES = 128
ROUTE_COLS = LANES
EXPERT_COL0 = N_GROUPS

TM_PROJ = 512
TQ_ATTN = MOBA_BLOCK
TQ_CROSS = 512
TM_EXPERT = 256
VMEM_LIMIT = 56 * 1024 * 1024


def _alibi_slopes(n):
    return [2.0 ** (-8.0 * (i + 1) / n) for i in range(n)]


def _rms(x, g):
    y = x * lax.rsqrt(jnp.mean(x * x, axis=-1, keepdims=True) + RMS_EPS)
    return y * g


def _dot_nt(a, b):
    return lax.dot_general(a, b, (((1,), (1,)), ((), ())), preferred_element_type=F32)


def _params(*sem):
    return pltpu.CompilerParams(dimension_semantics=sem, vmem_limit_bytes=VMEM_LIMIT)


def _inproj_kernel(x_ref, g_ref, w_ref, *out_refs, scales):
    xn = _rms(x_ref[...], g_ref[...]).astype(BF16)
    width = out_refs[0].shape[1]
    for i, o_ref in enumerate(out_refs):
        y = jnp.dot(xn, w_ref[:, i * width:(i + 1) * width], preferred_element_type=F32)
        o_ref[...] = (y if scales[i] == 1.0 else y * scales[i]).astype(BF16)


def _inproj(x2, g, w, scales):
    t, d = x2.shape
    n_out = w.shape[1] // MOBA_WIDTH
    return pl.pallas_call(
        functools.partial(_inproj_kernel, scales=scales),
        grid=(t // TM_PROJ,),
        in_specs=[pl.BlockSpec((TM_PROJ, d), lambda i: (i, 0)),
                  pl.BlockSpec((1, d), lambda i: (0, 0)),
                  pl.BlockSpec(w.shape, lambda i: (0, 0))],
        out_specs=[pl.BlockSpec((TM_PROJ, MOBA_WIDTH), lambda i: (i, 0))] * n_out,
        out_shape=[jax.ShapeDtypeStruct((t, MOBA_WIDTH), BF16)] * n_out,
        compiler_params=_params("arbitrary"),
        name="inproj",
    )(x2, g, w)


AUX_PEN0 = 0
AUX_BLK0 = 64
AUX_OFF0 = 68
AUX_ROW = 72
N_SPLIT = 4


def _bf16_pieces(c):
    pieces, rem = [], np.float64(c)
    for _ in range(N_SPLIT):
        p = np.float64(np.asarray(rem).astype(BF16))
        pieces.append(p)
        rem = rem - p
    return pieces


def _key_aux(seq):
    pos = np.arange(seq)
    blk, off = pos // MOBA_BLOCK, pos % MOBA_BLOCK
    assert blk.max() < 8, "penalty lanes hold 8 blocks per head"
    a = np.zeros((seq, LANES), np.float32)
    for h in range(MOBA_HEADS):
        a[pos, AUX_PEN0 + h * 8 + blk] = 1.0
    a[:, AUX_BLK0:AUX_BLK0 + N_SPLIT] = (blk * MOBA_BLOCK)[:, None]
    a[:, AUX_OFF0:AUX_OFF0 + N_SPLIT] = off[:, None]
    a[:, AUX_ROW] = 1.0
    return jnp.asarray(a, BF16)


def _value_aux():
    a = np.zeros((2 * MOBA_BLOCK, LANES), np.float32)
    a[:, 0] = 1.0
    return jnp.asarray(a, BF16)


def _query_aux_rows(slopes):
    a = np.zeros((len(slopes), LANES), np.float32)
    for i, s in enumerate(slopes):
        a[i, AUX_BLK0:AUX_BLK0 + N_SPLIT] = _bf16_pieces(s * LOG2E)
        a[i, AUX_OFF0:AUX_OFF0 + N_SPLIT] = _bf16_pieces(s * LOG2E)
    return jnp.asarray(a)


def _query_aux(crow, slope, j, pen_t=None, pen_lo=0):
    blk = MOBA_BLOCK
    lane = lax.broadcasted_iota(jnp.int32, (blk, LANES), 1)
    t_q = (j * blk).astype(F32) + lax.broadcasted_iota(jnp.int32, (blk, LANES), 0).astype(F32)
    aux = jnp.where(lane == AUX_ROW, (-slope * LOG2E) * t_q, crow)
    if pen_t is not None:
        aux = jnp.where((lane >= pen_lo) & (lane < pen_lo + 8), pen_t, aux)
    return aux.astype(BF16)


def _attend(qaug_ref, k_ref, v_ref, kaux_ref, vaux_ref, s_ref, sown_ref, mrun_ref, acc_ref, j, kcols, vcols):
    n_items = qaug_ref.shape[0]
    blk = MOBA_BLOCK
    causal = (lax.broadcasted_iota(jnp.int32, (blk, blk), 0) >= lax.broadcasted_iota(jnp.int32, (blk, blk), 1))
    own = pl.ds(pl.multiple_of(j * blk, blk), blk)

    def scores(i, rows, kaux):
        return _dot_nt(qaug_ref[i], jnp.concatenate([k_ref[rows, kcols[i]], kaux], axis=1))

    def half_max(s):
        return jnp.maximum(s[:, :LANES], s[:, LANES:])

    kaux_own = kaux_ref[own, :]
    for i in range(n_items):
        s = jnp.where(causal, scores(i, own, kaux_own), NEG_INF)
        sown_ref[i] = s
        mrun_ref[i] = half_max(s)

    def pass_a(n, width):
        rows = pl.ds(pl.multiple_of(n * blk, blk), width * blk)
        kaux_n = kaux_ref[rows, :]
        for i in range(n_items):
            s = scores(i, rows, kaux_n)
            m = mrun_ref[i]
            for w in range(width):
                sw = s[:, w * blk:(w + 1) * blk]
                s_ref[i, n + w] = sw
                m = jnp.maximum(m, half_max(sw))
            mrun_ref[i] = m

    def loop_past(fn):
        def pair(t, c):
            fn(2 * t, 2)
            return c
        lax.fori_loop(0, j // 2, pair, 0)

        @pl.when(j % 2 == 1)
        def _():
            fn(j - 1, 1)

    loop_past(pass_a)

    for i in range(n_items):
        mrun_ref[i] = jnp.broadcast_to(jnp.max(mrun_ref[i], axis=1, keepdims=True), (blk, LANES))

    def probs(i, s):
        m = mrun_ref[i]
        return jnp.concatenate([jnp.exp2(s[:, :LANES] - m), jnp.exp2(s[:, LANES:] - m)], axis=1).astype(BF16)

    def pv(i, p, rows, width):
        v_aug = jnp.concatenate([v_ref[rows, vcols[i]], vaux_ref[:width * blk, :]], axis=1)
        return jnp.dot(p, v_aug, preferred_element_type=F32)

    for i in range(n_items):
        acc_ref[i] = pv(i, probs(i, sown_ref[i]), own, 1)

    def pass_b(n, width):
        rows = pl.ds(pl.multiple_of(n * blk, blk), width * blk)
        for i in range(n_items):
            p = jnp.concatenate([probs(i, s_ref[i, n + w]) for w in range(width)], axis=1)
            acc_ref[i] += pv(i, p, rows, width)

    loop_past(pass_b)

    outs = []
    for i in range(n_items):
        a = acc_ref[i]
        outs.append(a[:, :LANES] * (1.0 / a[:, LANES:LANES + 1]))
    return outs


def _attend_scratch(n_items, nblk):
    blk = MOBA_BLOCK
    return [pltpu.VMEM((n_items, blk, 2 * LANES), BF16),
            pltpu.VMEM((n_items, nblk, blk, blk), F32),
            pltpu.VMEM((n_items, blk, blk), F32),
            pltpu.VMEM((n_items, blk, LANES), F32),
            pltpu.VMEM((n_items, blk, 2 * LANES), F32)]


def _moba_kernel(q_ref, k_ref, v_ref, kaux_ref, vaux_ref, crow_ref, g_ref, o_ref,
                 kmean_ref, kbd_ref, pen_ref, qaug_ref, s_ref, sown_ref, mrun_ref, acc_ref):
    j = pl.program_id(1)
    blk = MOBA_BLOCK
    nblk = s_ref.shape[1]
    slopes = _alibi_slopes(MOBA_HEADS)
    n_slabs = MOBA_WIDTH // LANES

    @pl.when(j == 0)
    def _():
        kmean_ref[...] = jnp.zeros_like(kmean_ref)
        for n in range(nblk):
            kb = k_ref[n * blk:(n + 1) * blk, :].astype(F32)
            kmean_ref[n:n + 1, :] = jnp.sum(kb, axis=0, keepdims=True) * (1.0 / blk)
        col_head = lax.broadcasted_iota(jnp.int32, kmean_ref.shape, 1) // HEAD_DIM
        for h in range(MOBA_HEADS):
            kbd_ref[h * 8:(h + 1) * 8, :] = jnp.where(col_head == h, kmean_ref[...], 0.0)

    pen_ref[...] = jnp.zeros_like(pen_ref)

    @pl.when(j > MOBA_TOPK)
    def _():
        kbd = kbd_ref[...]
        kbd_hi = kbd.astype(BF16)
        kbd_lo = (kbd - kbd_hi.astype(F32)).astype(BF16)
        gate = _dot_nt(kbd_hi, q_ref[...]) + _dot_nt(kbd_lo, q_ref[...])
        n_iota = lax.broadcasted_iota(jnp.int32, (8, blk), 0)
        for h in range(MOBA_HEADS):
            g = gate[h * 8:(h + 1) * 8, :]
            rank = jnp.zeros((8, blk), jnp.int32)
            for m in range(nblk):
                gm = g[m:m + 1, :]
                ahead = (gm > g) | ((gm == g) & (m < n_iota))
                rank = rank + jnp.where(ahead, 1, 0) * (m < j).astype(jnp.int32)
            pen_ref[h * 8:(h + 1) * 8, :] = jnp.where((n_iota < j) & (rank >= MOBA_TOPK), NEG_INF, 0.0)

    pen_t = pen_ref[...].T

    lane = lax.broadcasted_iota(jnp.int32, (blk, LANES), 1)
    for h in range(MOBA_HEADS):
        half = h % 2
        q_slab = q_ref[:, (h // 2) * LANES:(h // 2 + 1) * LANES]
        in_head = (lane >= half * HEAD_DIM) & (lane < (half + 1) * HEAD_DIM)
        aux = _query_aux(crow_ref[h:h + 1, :], slopes[h], j, pen_t, AUX_PEN0 + h * 8)
        qaug_ref[h] = jnp.concatenate([jnp.where(in_head, q_slab, 0).astype(BF16), aux], axis=1)

    cols = [slice((h // 2) * LANES, (h // 2 + 1) * LANES) for h in range(MOBA_HEADS)]
    outs = _attend(qaug_ref, k_ref, v_ref, kaux_ref, vaux_ref, s_ref, sown_ref, mrun_ref, acc_ref, j, cols, cols)
    o = jnp.concatenate([jnp.where(lane < HEAD_DIM, outs[2 * p], outs[2 * p + 1]) for p in range(n_slabs)], axis=1)
    o_ref[...] = _rms(o, g_ref[...]).astype(BF16)


def _moba(qa, ka, va, key_aux, val_aux, g, batch, seq):
    nblk = seq // MOBA_BLOCK
    w = MOBA_WIDTH
    crow = _query_aux_rows(_alibi_slopes(MOBA_HEADS))
    whole = lambda arr: pl.BlockSpec(arr.shape, lambda b, j: (0, 0))
    return pl.pallas_call(
        _moba_kernel,
        grid=(batch, nblk),
        in_specs=[pl.BlockSpec((MOBA_BLOCK, w), lambda b, j: (b * nblk + j, 0)),
                  pl.BlockSpec((seq, w), lambda b, j: (b, 0)),
                  pl.BlockSpec((seq, w), lambda b, j: (b, 0)),
                  whole(key_aux), whole(val_aux), whole(crow), whole(g)],
        out_specs=pl.BlockSpec((MOBA_BLOCK, w), lambda b, j: (b * nblk + j, 0)),
        out_shape=jax.ShapeDtypeStruct((batch * seq, w), BF16),
        scratch_shapes=[pltpu.VMEM((8, w), F32), pltpu.VMEM((8 * MOBA_HEADS, w), F32),
                        pltpu.VMEM((LANES, MOBA_BLOCK), F32)]
        + _attend_scratch(MOBA_HEADS, nblk),
        compiler_params=_params("arbitrary", "arbitrary"),
        name="moba",
    )(qa, ka, va, key_aux, val_aux, crow, g)


def _diff_kernel(q_ref, k_ref, v_ref, kaux_ref, vaux_ref, crow_ref, lam_ref, g_ref, o_ref,
                 qaug_ref, s_ref, sown_ref, mrun_ref, acc_ref, *, lam_init):
    j = pl.program_id(1)
    blk = MOBA_BLOCK
    slopes = _alibi_slopes(DIFF_HEADS)
    lane = lax.broadcasted_iota(jnp.int32, (blk, LANES), 1)

    lv = lam_ref[...]
    lam = (jnp.exp(jnp.sum(lv[0:1] * lv[1:2], axis=1, keepdims=True))
           - jnp.exp(jnp.sum(lv[2:3] * lv[3:4], axis=1, keepdims=True)) + lam_init)

    for i in range(2 * DIFF_HEADS):
        h, c = i // 2, i % 2
        q_slab = q_ref[:, h * LANES:(h + 1) * LANES]
        in_map = (lane >= c * HEAD_DIM) & (lane < (c + 1) * HEAD_DIM)
        aux = _query_aux(crow_ref[i:i + 1, :], slopes[h], j)
        qaug_ref[i] = jnp.concatenate([jnp.where(in_map, q_slab, 0).astype(BF16), aux], axis=1)

    cols = [slice((i // 2) * LANES, (i // 2 + 1) * LANES) for i in range(2 * DIFF_HEADS)]
    outs = _attend(qaug_ref, k_ref, v_ref, kaux_ref, vaux_ref, s_ref, sown_ref, mrun_ref, acc_ref, j, cols, cols)
    for h in range(DIFF_HEADS):
        o = outs[2 * h] - lam * outs[2 * h + 1]
        o_ref[:, h * LANES:(h + 1) * LANES] = (_rms(o, g_ref[...]) * (1.0 - lam_init)).astype(BF16)


def _diff(qd, kd, vd, key_aux, val_aux, lams, g, batch, seq, lam_init):
    nblk = seq // MOBA_BLOCK
    w = DIFF_WIDTH
    crow = _query_aux_rows([s for s in _alibi_slopes(DIFF_HEADS) for _ in range(2)])
    whole = lambda arr: pl.BlockSpec(arr.shape, lambda b, j: (0, 0))
    return pl.pallas_call(
        functools.partial(_diff_kernel, lam_init=lam_init),
        grid=(batch, nblk),
        in_specs=[pl.BlockSpec((MOBA_BLOCK, w), lambda b, j: (b * nblk + j, 0)),
                  pl.BlockSpec((seq, w), lambda b, j: (b, 0)),
                  pl.BlockSpec((seq, w), lambda b, j: (b, 0)),
                  whole(key_aux), whole(val_aux), whole(crow), whole(lams), whole(g)],
        out_specs=pl.BlockSpec((MOBA_BLOCK, w), lambda b, j: (b * nblk + j, 0)),
        out_shape=jax.ShapeDtypeStruct((batch * seq, w), BF16),
        scratch_shapes=_attend_scratch(2 * DIFF_HEADS, nblk),
        compiler_params=_params("arbitrary", "arbitrary"),
        name="diff",
    )(qd, kd, vd, key_aux, val_aux, crow, lams, g)


def _outproj_kernel(a_ref, d_ref, x_ref, wo_ref, g_ref, wq_ref, h_ref, q_ref, *, q_scale):
    wa = a_ref.shape[1]
    h = (x_ref[...]
         + jnp.dot(a_ref[...], wo_ref[:wa, :], preferred_element_type=F32)
         + jnp.dot(d_ref[...], wo_ref[wa:, :], preferred_element_type=F32))
    h_ref[...] = h
    qn = _rms(h, g_ref[...]).astype(BF16)
    q_ref[...] = (jnp.dot(qn, wq_ref[...], preferred_element_type=F32) * q_scale).astype(BF16)


def _outproj(a, d, x2, w_out, g, w_q, q_scale):
    t, dm = x2.shape
    tile = lambda w: pl.BlockSpec((TM_PROJ, w), lambda i: (i, 0))
    whole = lambda arr: pl.BlockSpec(arr.shape, lambda i: (0, 0))
    return pl.pallas_call(
        functools.partial(_outproj_kernel, q_scale=q_scale),
        grid=(t // TM_PROJ,),
        in_specs=[tile(a.shape[1]), tile(d.shape[1]), tile(dm), whole(w_out), whole(g), whole(w_q)],
        out_specs=[tile(dm), tile(dm)],
        out_shape=[jax.ShapeDtypeStruct((t, dm), F32), jax.ShapeDtypeStruct((t, dm), BF16)],
        compiler_params=_params("arbitrary"),
        name="outproj",
    )(a, d, x2, w_out, g, w_q)


def _memkv_kernel(m_ref, g_ref, w_ref, kv_ref):
    mn = _rms(m_ref[...], g_ref[...]).astype(BF16)
    kv_ref[...] = jnp.dot(mn, w_ref[...], preferred_element_type=F32).astype(BF16)


def _memkv(mem2, g, w_kv, mem_len):
    rows, dm = mem2.shape
    return pl.pallas_call(
        _memkv_kernel,
        grid=(rows // mem_len,),
        in_specs=[pl.BlockSpec((mem_len, dm), lambda i: (i, 0)),
                  pl.BlockSpec((1, dm), lambda i: (0, 0)),
                  pl.BlockSpec(w_kv.shape, lambda i: (0, 0))],
        out_specs=pl.BlockSpec((mem_len, w_kv.shape[1]), lambda i: (i, 0)),
        out_shape=jax.ShapeDtypeStruct((rows, w_kv.shape[1]), BF16),
        compiler_params=_params("arbitrary"),
        name="memkv",
    )(mem2, g, w_kv)


def _cross_kernel(q_ref, kv_ref, h_ref, wo_ref, g_ref, wr_ref, br_ref,
                  h2_ref, xn_ref, route_ref, counts_ref, run_ref):
    first = (pl.program_id(0) == 0) & (pl.program_id(1) == 0)

    @pl.when(first)
    def _():
        run_ref[...] = jnp.zeros_like(run_ref)

    tq, dm = h_ref.shape
    dh = dm // MEM_HEADS
    heads = []
    for h in range(MEM_HEADS):
        s = _dot_nt(q_ref[:, h * dh:(h + 1) * dh], kv_ref[:, h * dh:(h + 1) * dh])
        m = jnp.max(s, axis=-1, keepdims=True)
        e = jnp.exp(s - m)
        l = jnp.sum(e, axis=-1, keepdims=True)
        o = jnp.dot(e.astype(BF16), kv_ref[:, dm + h * dh:dm + (h + 1) * dh], preferred_element_type=F32)
        heads.append((o * (1.0 / l)).astype(BF16))
    o = jnp.concatenate(heads, axis=1)
    h2 = h_ref[...] + jnp.dot(o, wo_ref[...], preferred_element_type=F32)
    h2_ref[...] = h2
    xn = _rms(h2, g_ref[...])
    _to_token_major(xn_ref, xn)

    x_hi = xn.astype(BF16)
    x_lo = (xn - x_hi.astype(F32)).astype(BF16)
    hi = jnp.dot(x_hi, wr_ref[...], preferred_element_type=F32)
    lo_hi = jnp.dot(x_lo, wr_ref[:, :ROUTE_COLS], preferred_element_type=F32)
    logits = hi[:, :ROUTE_COLS] + (hi[:, ROUTE_COLS:] + lo_hi) + br_ref[...]
    lane = lax.broadcasted_iota(jnp.int32, logits.shape, 1).astype(F32)
    big = float(ROUTE_COLS)

    def top1(vals):
        v = jnp.max(vals, axis=-1, keepdims=True)
        i = jnp.min(jnp.where(vals == v, lane, big), axis=-1, keepdims=True)
        return v, i

    gl = jnp.where(lane < N_GROUPS, logits, -jnp.inf)
    g_max, g_idx = top1(gl)
    g_w = 1.0 / jnp.sum(jnp.exp(gl - g_max), axis=-1, keepdims=True)
    lo = EXPERT_COL0 + g_idx * EXPERTS_PER_GROUP
    el = jnp.where((lane >= lo) & (lane < lo + EXPERTS_PER_GROUP), logits, -jnp.inf)
    v1, i1 = top1(el)
    v2, i2 = top1(jnp.where(lane == i1, -jnp.inf, el))
    e2 = jnp.exp(v2 - v1)
    w1 = g_w * (1.0 / (1.0 + e2))
    w2 = g_w * (e2 / (1.0 + e2))

    onehot = jnp.where((lane == i1) | (lane == i2), 1.0, 0.0)
    run_ref[...] = run_ref[...] + jnp.sum(onehot, axis=0, keepdims=True)
    counts_ref[...] = run_ref[...]

    rec = jnp.zeros(logits.shape, F32)
    for col, val in enumerate((i1 - EXPERT_COL0, i2 - EXPERT_COL0, w1, w2)):
        rec = jnp.where(lane == col, val, rec)
    route_ref[...] = rec


def _cross(qc, kv, h1, w_o, g, w_r, b_r, batch, seq, mem_len):
    t, dm = h1.shape
    nt = seq // TQ_CROSS
    tile = lambda w: pl.BlockSpec((TQ_CROSS, w), lambda b, i: (b * nt + i, 0))
    whole = lambda arr: pl.BlockSpec(arr.shape, lambda b, i: (0, 0))
    return pl.pallas_call(
        _cross_kernel,
        grid=(batch, nt),
        in_specs=[tile(dm), pl.BlockSpec((mem_len, kv.shape[1]), lambda b, i: (b, 0)), tile(dm),
                  whole(w_o), whole(g), whole(w_r), whole(b_r)],
        out_specs=[tile(dm), pl.BlockSpec((TQ_CROSS * SUB, LANES), lambda b, i: (b * nt + i, 0)),
                   tile(ROUTE_COLS), pl.BlockSpec((1, ROUTE_COLS), lambda b, i: (0, 0))],
        out_shape=[jax.ShapeDtypeStruct((t, dm), F32), jax.ShapeDtypeStruct((t * SUB, LANES), F32),
                   jax.ShapeDtypeStruct((t, ROUTE_COLS), F32), jax.ShapeDtypeStruct((1, ROUTE_COLS), F32)],
        scratch_shapes=[pltpu.VMEM((1, ROUTE_COLS), F32)],
        compiler_params=_params("arbitrary", "arbitrary"),
        name="cross",
    )(qc, kv, h1, w_o, g, w_r, b_r)


SUB = 8


def _to_token_major(ref, x):
    rows = x.shape[0]
    for c in range(x.shape[1] // LANES):
        ref[pl.ds(c, rows, stride=SUB), :] = x[:, c * LANES:(c + 1) * LANES]


def _from_token_major(ref, rows):
    n_chunks = ref.shape[0] // rows
    return jnp.concatenate([ref[pl.ds(c, rows, stride=SUB), :] for c in range(n_chunks)], axis=1)


def _experts_kernel(te_ref, nused_ref, cs_ref, nv_ref, rows_ref, xn_ref, wg_ref, wu_ref, wd_ref, yo_ref,
                    xbuf0, xbuf1, ybuf0, ybuf1, wgu_bf, wd_bf, gsem, ssem, *, n_tok):
    i = pl.program_id(0)
    last = pl.num_programs(0) - 1
    n_used = nused_ref[0]
    xbuf, ybuf = (xbuf0, xbuf1), (ybuf0, ybuf1)
    tm = xbuf0.shape[0] // SUB
    ff = wd_ref.shape[0]
    token_rows = lambda tok: pl.ds(pl.multiple_of(tok * SUB, SUB), SUB)

    def gather(tile, buf):
        base = cs_ref[tile]
        for s in range(tm):
            row = rows_ref[jnp.minimum(base + s, 2 * n_tok - 1)]
            pltpu.make_async_copy(xn_ref.at[token_rows(row & (n_tok - 1))], xbuf[buf].at[token_rows(s)],
                                  gsem.at[buf]).start()

    def scatter(tile, buf):
        base = cs_ref[tile]

        def copy(s):
            return pltpu.make_async_copy(ybuf[buf].at[token_rows(s)], yo_ref.at[token_rows(rows_ref[base + s])],
                                         ssem.at[buf])

        @pl.when(nv_ref[tile] == tm)
        def _():
            for s in range(tm):
                copy(s).start()

        @pl.when(nv_ref[tile] < tm)
        def _():
            def body(s, c):
                copy(s).start()
                return c
            lax.fori_loop(0, nv_ref[tile], body, 0)

    @pl.when((i < n_used) & ((i == 0) | (te_ref[i] != te_ref[jnp.maximum(i - 1, 0)])))
    def _():
        wgu_bf[:, :ff] = wg_ref[...].astype(BF16)
        wgu_bf[:, ff:] = wu_ref[...].astype(BF16)
        wd_bf[...] = wd_ref[...].astype(BF16)

    def wait_gather(buf):
        pltpu.make_async_copy(xn_ref.at[pl.ds(0, tm * SUB)], xbuf[buf], gsem.at[buf]).wait()

    def wait_scatter(tile, buf):
        @pl.when(nv_ref[tile] == tm)
        def _():
            pltpu.make_async_copy(ybuf[buf], yo_ref.at[pl.ds(0, tm * SUB)], ssem.at[buf]).wait()

        @pl.when(nv_ref[tile] < tm)
        def _():
            def body(s, c):
                pltpu.make_async_copy(ybuf[buf].at[pl.ds(0, SUB)], yo_ref.at[pl.ds(0, SUB)], ssem.at[buf]).wait()
                return c
            lax.fori_loop(0, nv_ref[tile], body, 0)

    def step(cur):
        nxt = 1 - cur

        @pl.when((i == 0) & (n_used > 0))
        def _():
            gather(0, cur)

        @pl.when((i >= 2) & (i - 2 < n_used))
        def _():
            wait_scatter(i - 2, cur)

        @pl.when(i < n_used)
        def _():
            wait_gather(cur)
            gather(jnp.minimum(i + 1, n_used - 1), nxt)
            x = _from_token_major(xbuf[cur], tm).astype(BF16)
            gu = jnp.dot(x, wgu_bf[...], preferred_element_type=F32)
            g = gu[:, :ff]
            hh = (g * jax.nn.sigmoid(g)) * gu[:, ff:]
            _to_token_major(ybuf[cur], jnp.dot(hh.astype(BF16), wd_bf[...], preferred_element_type=F32))
            scatter(i, cur)

            @pl.when(i + 1 == n_used)
            def _():
                wait_gather(nxt)

        @pl.when(i == last)
        def _():
            @pl.when((i >= 1) & (i - 1 < n_used))
            def _():
                wait_scatter(i - 1, nxt)

            @pl.when(i < n_used)
            def _():
                wait_scatter(i, cur)

    for parity in range(2):
        pl.when(i % 2 == parity)(functools.partial(step, parity))


def _experts(tile_expert, n_used, tile_start, tile_valid, sorted_rows, xn_tm, w_gate, w_up, w_down, tm):
    n_tok = xn_tm.shape[0] // SUB
    n_tiles = tile_expert.shape[0]
    _, dm, ff = w_gate.shape
    assert n_tok & (n_tok - 1) == 0, "source token = output row & (n_tok - 1)"
    assert dm == SUB * LANES
    w_map = lambda i, te, nu, cs, nv, sr: (te[i], 0, 0)
    return pl.pallas_call(
        functools.partial(_experts_kernel, n_tok=n_tok),
        grid_spec=pltpu.PrefetchScalarGridSpec(
            num_scalar_prefetch=5,
            grid=(n_tiles,),
            in_specs=[pl.BlockSpec(memory_space=pl.ANY),
                      pl.BlockSpec((None, dm, ff), w_map), pl.BlockSpec((None, dm, ff), w_map),
                      pl.BlockSpec((None, ff, dm), w_map)],
            out_specs=pl.BlockSpec(memory_space=pl.ANY),
            scratch_shapes=[pltpu.VMEM((tm * SUB, LANES), F32)] * 4
            + [pltpu.VMEM((dm, 2 * ff), BF16), pltpu.VMEM((ff, dm), BF16),
                            pltpu.SemaphoreType.DMA((2,)), pltpu.SemaphoreType.DMA((2,))]),
        out_shape=jax.ShapeDtypeStruct((2 * n_tok * SUB, LANES), F32),
        compiler_params=_params("arbitrary"),
        name="experts",
    )(tile_expert, n_used, tile_start, tile_valid, sorted_rows, xn_tm, w_gate, w_up, w_down)


def _combine_kernel(h2_ref, route_ref, y0_ref, y1_ref, g_ref, o_ref):
    rows = h2_ref.shape[0]
    rec = route_ref[...]
    h = (h2_ref[...] + rec[:, 2:3] * _from_token_major(y0_ref, rows)
         + rec[:, 3:4] * _from_token_major(y1_ref, rows))
    o_ref[...] = _rms(h, g_ref[...])


def _combine(h2, route, yo, g):
    t, dm = h2.shape
    nt = t // TM_PROJ
    tile = lambda w: pl.BlockSpec((TM_PROJ, w), lambda i: (i, 0))
    return pl.pallas_call(
        _combine_kernel,
        grid=(nt,),
        in_specs=[tile(dm), tile(ROUTE_COLS),
                  pl.BlockSpec((TM_PROJ * SUB, LANES), lambda i: (i, 0)),
                  pl.BlockSpec((TM_PROJ * SUB, LANES), lambda i: (i + nt, 0)),
                  pl.BlockSpec((1, dm), lambda i: (0, 0))],
        out_specs=tile(dm),
        out_shape=jax.ShapeDtypeStruct((t, dm), F32),
        compiler_params=_params("arbitrary"),
        name="combine",
    )(h2, route, yo, yo, g)


def kernel(x, mem, norm_mix, w_in, lambda_q1, lambda_k1, lambda_q2, lambda_k2, diff_subln, norm_moba_out, w_out, norm_mem_q, norm_mem_kv, w_mem_q, w_mem_kv, w_mem_o, norm_ffn, w_router_group, b_router_group, w_router_expert, b_router_expert, w_expert_gate, w_expert_up, w_expert_down, norm_final):
    batch, seq, dm = x.shape
    mem_len = mem.shape[1]
    depth = w_in.shape[0]
    t = batch * seq
    assert seq % MOBA_BLOCK == 0 and seq % TQ_CROSS == 0 and t % TM_PROJ == 0

    h = x.reshape(t, dm)
    mem2 = mem.reshape(batch * mem_len, dm)
    row = lambda v: v.reshape(1, -1).astype(F32)
    for l in range(depth):
        lam_init = 0.8 - 0.6 * math.exp(-0.3 * l)
        q_scale = HEAD_DIM ** -0.5 * LOG2E
        qa, ka, va, qd, kd, vd = _inproj(h, row(norm_mix[l]), w_in[l].astype(BF16),
                                         (q_scale, 1.0, 1.0, q_scale, 1.0, 1.0))
        key_aux, val_aux = _key_aux(seq), _value_aux()
        a = _moba(qa, ka, va, key_aux, val_aux, row(norm_moba_out[l]), batch, seq)
        lams = jnp.stack([lambda_q1[l], lambda_k1[l], lambda_q2[l], lambda_k2[l]]).astype(F32)
        d = _diff(qd, kd, vd, key_aux, val_aux, lams, row(diff_subln[l]), batch, seq, lam_init)
        h1, qc = _outproj(a, d, h, w_out[l].astype(BF16), row(norm_mem_q[l]), w_mem_q[l].astype(BF16),
                          (dm // MEM_HEADS) ** -0.5)
        kv = _memkv(mem2, row(norm_mem_kv[l]), w_mem_kv[l].astype(BF16), mem_len)

        pad = ROUTE_COLS - N_GROUPS - N_EXPERTS
        w_r = jnp.pad(jnp.concatenate([w_router_group[l], w_router_expert[l]], axis=1).astype(F32), ((0, 0), (0, pad)))
        w_r_hi = w_r.astype(BF16)
        w_r = jnp.concatenate([w_r_hi, (w_r - w_r_hi.astype(F32)).astype(BF16)], axis=1)
        b_r = jnp.pad(jnp.concatenate([b_router_group[l], b_router_expert[l]]).astype(F32), (0, pad)).reshape(1, -1)
        h2, xn, route, counts = _cross(qc, kv, h1, w_mem_o[l].astype(BF16), row(norm_ffn[l]), w_r, b_r,
                                       batch, seq, mem_len)

        tm = TM_EXPERT
        n_tiles = (2 * t) // tm + N_EXPERTS
        cnt = counts[0, EXPERT_COL0:EXPERT_COL0 + N_EXPERTS].astype(jnp.int32)
        tiles_of = (cnt + tm - 1) // tm
        tile_end = jnp.cumsum(tiles_of)
        tile_idx = jnp.arange(n_tiles, dtype=jnp.int32)
        tile_expert = jnp.minimum(jnp.sum((tile_end[None, :] <= tile_idx[:, None]).astype(jnp.int32), axis=1),
                                  N_EXPERTS - 1)
        n_used = tile_end[-1:].astype(jnp.int32)
        onehot_te = (tile_expert[:, None] == jnp.arange(N_EXPERTS, dtype=jnp.int32)[None, :]).astype(jnp.int32)
        pick = lambda table: jnp.sum(onehot_te * table[None, :], axis=1)
        in_expert = (tile_idx - pick(tile_end - tiles_of)) * tm
        tile_start = pick(jnp.cumsum(cnt) - cnt) + in_expert
        tile_valid = jnp.where(tile_idx < n_used[0], jnp.clip(pick(cnt) - in_expert, 0, tm), 0).astype(jnp.int32)
        out_row = jnp.arange(t, dtype=jnp.int32)[:, None] + jnp.array([0, t], jnp.int32)[None, :]
        keys = route[:, 0:2].astype(jnp.int32) * (2 * t) + out_row
        sorted_rows = jnp.sort(keys.reshape(-1)) & (2 * t - 1)

        yo = _experts(tile_expert, n_used, tile_start.astype(jnp.int32), tile_valid, sorted_rows, xn,
                      w_expert_gate[l].astype(F32), w_expert_up[l].astype(F32), w_expert_down[l].astype(F32), tm)
        assert depth == 1
        h = _combine(h2, route, yo, row(norm_final))
    return h.reshape(batch, seq, dm)
```

```python
import functools
import math

import jax
import jax.numpy as jnp
import numpy as np
from jax import lax
from jax.experimental import pallas as pl
from jax.experimental.pallas import tpu as pltpu

F32 = jnp.float32
BF16 = jnp.bfloat16

HEAD_DIM = 64
MOBA_HEADS = 8
MOBA_WIDTH = MOBA_HEADS * HEAD_DIM
MOBA_BLOCK = 256
MOBA_TOPK = 3
DIFF_HEADS = 4
DIFF_V_DIM = 2 * HEAD_DIM
DIFF_WIDTH = DIFF_HEADS * DIFF_V_DIM
MEM_HEADS = 4
N_GROUPS = 4
EXPERTS_PER_GROUP = 8
N_EXPERTS = N_GROUPS * EXPERTS_PER_GROUP
RMS_EPS = 1e-6
NEG_INF = -1e30
LOG2E = math.log2(math.e)
LANES = 128
ROUTE_COLS = LANES
EXPERT_COL0 = N_GROUPS

TM_PROJ = 1024
TQ_ATTN = MOBA_BLOCK
TQ_CROSS = 512
TM_EXPERT = 256
VMEM_LIMIT = 56 * 1024 * 1024


def _alibi_slopes(n):
    return [2.0 ** (-8.0 * (i + 1) / n) for i in range(n)]


def _rms(x, g):
    y = x * lax.rsqrt(jnp.mean(x * x, axis=-1, keepdims=True) + RMS_EPS)
    return y * g


def _dot_nt(a, b):
    return lax.dot_general(a, b, (((1,), (1,)), ((), ())), preferred_element_type=F32)


def _params(*sem):
    return pltpu.CompilerParams(dimension_semantics=sem, vmem_limit_bytes=VMEM_LIMIT)


def _inproj_kernel(x_ref, g_ref, w_ref, *out_refs, scales):
    xn = _rms(x_ref[...], g_ref[...]).astype(BF16)
    width = out_refs[0].shape[1]
    for i, o_ref in enumerate(out_refs):
        y = jnp.dot(xn, w_ref[:, i * width:(i + 1) * width], preferred_element_type=F32)
        o_ref[...] = (y if scales[i] == 1.0 else y * scales[i]).astype(BF16)


def _inproj(x2, g, w, scales):
    t, d = x2.shape
    n_out = w.shape[1] // MOBA_WIDTH
    return pl.pallas_call(
        functools.partial(_inproj_kernel, scales=scales),
        grid=(t // TM_PROJ,),
        in_specs=[pl.BlockSpec((TM_PROJ, d), lambda i: (i, 0)),
                  pl.BlockSpec((1, d), lambda i: (0, 0)),
                  pl.BlockSpec(w.shape, lambda i: (0, 0))],
        out_specs=[pl.BlockSpec((TM_PROJ, MOBA_WIDTH), lambda i: (i, 0))] * n_out,
        out_shape=[jax.ShapeDtypeStruct((t, MOBA_WIDTH), BF16)] * n_out,
        compiler_params=_params("arbitrary"),
        name="inproj",
    )(x2, g, w)


AUX_PEN0 = 0
AUX_BLK0 = 64
AUX_OFF0 = 68
AUX_ROW = 72
N_SPLIT = 4


def _bf16_pieces(c):
    pieces, rem = [], np.float64(c)
    for _ in range(N_SPLIT):
        p = np.float64(np.asarray(rem).astype(BF16))
        pieces.append(p)
        rem = rem - p
    return pieces


def _key_aux(seq):
    pos = np.arange(seq)
    blk, off = pos // MOBA_BLOCK, pos % MOBA_BLOCK
    assert blk.max() < 8, "penalty lanes hold 8 blocks per head"
    a = np.zeros((seq, LANES), np.float32)
    for h in range(MOBA_HEADS):
        a[pos, AUX_PEN0 + h * 8 + blk] = 1.0
    a[:, AUX_BLK0:AUX_BLK0 + N_SPLIT] = (blk * MOBA_BLOCK)[:, None]
    a[:, AUX_OFF0:AUX_OFF0 + N_SPLIT] = off[:, None]
    a[:, AUX_ROW] = 1.0
    return jnp.asarray(a, BF16)


def _value_aux():
    a = np.zeros((2 * MOBA_BLOCK, LANES), np.float32)
    a[:, 0] = 1.0
    return jnp.asarray(a, BF16)


def _query_aux_rows(slopes):
    a = np.zeros((len(slopes), LANES), np.float32)
    for i, s in enumerate(slopes):
        a[i, AUX_BLK0:AUX_BLK0 + N_SPLIT] = _bf16_pieces(s * LOG2E)
        a[i, AUX_OFF0:AUX_OFF0 + N_SPLIT] = _bf16_pieces(s * LOG2E)
    return jnp.asarray(a)


def _query_aux(crow, slope, j, pen_t=None, pen_lo=0):
    blk = MOBA_BLOCK
    lane = lax.broadcasted_iota(jnp.int32, (blk, LANES), 1)
    t_q = (j * blk).astype(F32) + lax.broadcasted_iota(jnp.int32, (blk, LANES), 0).astype(F32)
    aux = jnp.where(lane == AUX_ROW, (-slope * LOG2E) * t_q, crow)
    if pen_t is not None:
        aux = jnp.where((lane >= pen_lo) & (lane < pen_lo + 8), pen_t, aux)
    return aux.astype(BF16)


def _attend(qaug_ref, k_ref, v_ref, kaux_ref, vaux_ref, s_ref, sown_ref, mrun_ref, acc_ref, j, kcols, vcols):
    n_items = qaug_ref.shape[0]
    blk = MOBA_BLOCK
    causal = (lax.broadcasted_iota(jnp.int32, (blk, blk), 0) >= lax.broadcasted_iota(jnp.int32, (blk, blk), 1))
    own = pl.ds(pl.multiple_of(j * blk, blk), blk)

    def scores(i, rows, kaux):
        return _dot_nt(qaug_ref[i], jnp.concatenate([k_ref[rows, kcols[i]], kaux], axis=1))

    def half_max(s):
        return jnp.maximum(s[:, :LANES], s[:, LANES:])

    kaux_own = kaux_ref[own, :]
    for i in range(n_items):
        s = jnp.where(causal, scores(i, own, kaux_own), NEG_INF)
        sown_ref[i] = s
        mrun_ref[i] = half_max(s)

    def pass_a(n, width):
        rows = pl.ds(pl.multiple_of(n * blk, blk), width * blk)
        kaux_n = kaux_ref[rows, :]
        for i in range(n_items):
            s = scores(i, rows, kaux_n)
            m = mrun_ref[i]
            for w in range(width):
                sw = s[:, w * blk:(w + 1) * blk]
                s_ref[i, n + w] = sw
                m = jnp.maximum(m, half_max(sw))
            mrun_ref[i] = m

    def loop_past(fn):
        def pair(t, c):
            fn(2 * t, 2)
            return c
        lax.fori_loop(0, j // 2, pair, 0)

        @pl.when(j % 2 == 1)
        def _():
            fn(j - 1, 1)

    loop_past(pass_a)

    for i in range(n_items):
        mrun_ref[i] = jnp.broadcast_to(jnp.max(mrun_ref[i], axis=1, keepdims=True), (blk, LANES))

    def probs(i, s):
        m = mrun_ref[i]
        return jnp.concatenate([jnp.exp2(s[:, :LANES] - m), jnp.exp2(s[:, LANES:] - m)], axis=1).astype(BF16)

    def pv(i, p, rows, width):
        v_aug = jnp.concatenate([v_ref[rows, vcols[i]], vaux_ref[:width * blk, :]], axis=1)
        return jnp.dot(p, v_aug, preferred_element_type=F32)

    for i in range(n_items):
        acc_ref[i] = pv(i, probs(i, sown_ref[i]), own, 1)

    def pass_b(n, width):
        rows = pl.ds(pl.multiple_of(n * blk, blk), width * blk)
        for i in range(n_items):
            p = jnp.concatenate([probs(i, s_ref[i, n + w]) for w in range(width)], axis=1)
            acc_ref[i] += pv(i, p, rows, width)

    loop_past(pass_b)

    outs = []
    for i in range(n_items):
        a = acc_ref[i]
        outs.append(a[:, :LANES] * (1.0 / a[:, LANES:LANES + 1]))
    return outs


def _attend_scratch(n_items, nblk):
    blk = MOBA_BLOCK
    return [pltpu.VMEM((n_items, blk, 2 * LANES), BF16),
            pltpu.VMEM((n_items, nblk, blk, blk), F32),
            pltpu.VMEM((n_items, blk, blk), F32),
            pltpu.VMEM((n_items, blk, LANES), F32),
            pltpu.VMEM((n_items, blk, 2 * LANES), F32)]


def _moba_kernel(q_ref, k_ref, v_ref, kaux_ref, vaux_ref, crow_ref, g_ref, o_ref,
                 kmean_ref, kbd_ref, pen_ref, qaug_ref, s_ref, sown_ref, mrun_ref, acc_ref):
    j = pl.program_id(1)
    blk = MOBA_BLOCK
    nblk = s_ref.shape[1]
    slopes = _alibi_slopes(MOBA_HEADS)
    n_slabs = MOBA_WIDTH // LANES

    @pl.when(j == 0)
    def _():
        kmean_ref[...] = jnp.zeros_like(kmean_ref)
        for n in range(nblk):
            kb = k_ref[n * blk:(n + 1) * blk, :].astype(F32)
            kmean_ref[n:n + 1, :] = jnp.sum(kb, axis=0, keepdims=True) * (1.0 / blk)
        col_head = lax.broadcasted_iota(jnp.int32, kmean_ref.shape, 1) // HEAD_DIM
        for h in range(MOBA_HEADS):
            kbd_ref[h * 8:(h + 1) * 8, :] = jnp.where(col_head == h, kmean_ref[...], 0.0)

    pen_ref[...] = jnp.zeros_like(pen_ref)

    @pl.when(j > MOBA_TOPK)
    def _():
        kbd = kbd_ref[...]
        kbd_hi = kbd.astype(BF16)
        kbd_lo = (kbd - kbd_hi.astype(F32)).astype(BF16)
        gate = _dot_nt(kbd_hi, q_ref[...]) + _dot_nt(kbd_lo, q_ref[...])
        n_iota = lax.broadcasted_iota(jnp.int32, (8, blk), 0)
        for h in range(MOBA_HEADS):
            g = gate[h * 8:(h + 1) * 8, :]
            rank = jnp.zeros((8, blk), jnp.int32)
            for m in range(nblk):
                gm = g[m:m + 1, :]
                ahead = (gm > g) | ((gm == g) & (m < n_iota))
                rank = rank + jnp.where(ahead, 1, 0) * (m < j).astype(jnp.int32)
            pen_ref[h * 8:(h + 1) * 8, :] = jnp.where((n_iota < j) & (rank >= MOBA_TOPK), NEG_INF, 0.0)

    pen_t = pen_ref[...].T

    lane = lax.broadcasted_iota(jnp.int32, (blk, LANES), 1)
    for h in range(MOBA_HEADS):
        half = h % 2
        q_slab = q_ref[:, (h // 2) * LANES:(h // 2 + 1) * LANES]
        in_head = (lane >= half * HEAD_DIM) & (lane < (half + 1) * HEAD_DIM)
        aux = _query_aux(crow_ref[h:h + 1, :], slopes[h], j, pen_t, AUX_PEN0 + h * 8)
        qaug_ref[h] = jnp.concatenate([jnp.where(in_head, q_slab, 0).astype(BF16), aux], axis=1)

    cols = [slice((h // 2) * LANES, (h // 2 + 1) * LANES) for h in range(MOBA_HEADS)]
    outs = _attend(qaug_ref, k_ref, v_ref, kaux_ref, vaux_ref, s_ref, sown_ref, mrun_ref, acc_ref, j, cols, cols)
    o = jnp.concatenate([jnp.where(lane < HEAD_DIM, outs[2 * p], outs[2 * p + 1]) for p in range(n_slabs)], axis=1)
    o_ref[...] = _rms(o, g_ref[...]).astype(BF16)


def _moba(qa, ka, va, key_aux, val_aux, g, batch, seq):
    nblk = seq // MOBA_BLOCK
    w = MOBA_WIDTH
    crow = _query_aux_rows(_alibi_slopes(MOBA_HEADS))
    whole = lambda arr: pl.BlockSpec(arr.shape, lambda b, j: (0, 0))
    return pl.pallas_call(
        _moba_kernel,
        grid=(batch, nblk),
        in_specs=[pl.BlockSpec((MOBA_BLOCK, w), lambda b, j: (b * nblk + j, 0)),
                  pl.BlockSpec((seq, w), lambda b, j: (b, 0)),
                  pl.BlockSpec((seq, w), lambda b, j: (b, 0)),
                  whole(key_aux), whole(val_aux), whole(crow), whole(g)],
        out_specs=pl.BlockSpec((MOBA_BLOCK, w), lambda b, j: (b * nblk + j, 0)),
        out_shape=jax.ShapeDtypeStruct((batch * seq, w), BF16),
        scratch_shapes=[pltpu.VMEM((8, w), F32), pltpu.VMEM((8 * MOBA_HEADS, w), F32),
                        pltpu.VMEM((LANES, MOBA_BLOCK), F32)]
        + _attend_scratch(MOBA_HEADS, nblk),
        compiler_params=_params("arbitrary", "arbitrary"),
        name="moba",
    )(qa, ka, va, key_aux, val_aux, crow, g)


def _diff_kernel(q_ref, k_ref, v_ref, kaux_ref, vaux_ref, crow_ref, lam_ref, g_ref, o_ref,
                 qaug_ref, s_ref, sown_ref, mrun_ref, acc_ref, *, lam_init):
    j = pl.program_id(1)
    blk = MOBA_BLOCK
    slopes = _alibi_slopes(DIFF_HEADS)
    lane = lax.broadcasted_iota(jnp.int32, (blk, LANES), 1)

    lv = lam_ref[...]
    lam = (jnp.exp(jnp.sum(lv[0:1] * lv[1:2], axis=1, keepdims=True))
           - jnp.exp(jnp.sum(lv[2:3] * lv[3:4], axis=1, keepdims=True)) + lam_init)

    for i in range(2 * DIFF_HEADS):
        h, c = i // 2, i % 2
        q_slab = q_ref[:, h * LANES:(h + 1) * LANES]
        in_map = (lane >= c * HEAD_DIM) & (lane < (c + 1) * HEAD_DIM)
        aux = _query_aux(crow_ref[i:i + 1, :], slopes[h], j)
        qaug_ref[i] = jnp.concatenate([jnp.where(in_map, q_slab, 0).astype(BF16), aux], axis=1)

    cols = [slice((i // 2) * LANES, (i // 2 + 1) * LANES) for i in range(2 * DIFF_HEADS)]
    outs = _attend(qaug_ref, k_ref, v_ref, kaux_ref, vaux_ref, s_ref, sown_ref, mrun_ref, acc_ref, j, cols, cols)
    for h in range(DIFF_HEADS):
        o = outs[2 * h] - lam * outs[2 * h + 1]
        o_ref[:, h * LANES:(h + 1) * LANES] = (_rms(o, g_ref[...]) * (1.0 - lam_init)).astype(BF16)


def _diff(qd, kd, vd, key_aux, val_aux, lams, g, batch, seq, lam_init):
    nblk = seq // MOBA_BLOCK
    w = DIFF_WIDTH
    crow = _query_aux_rows([s for s in _alibi_slopes(DIFF_HEADS) for _ in range(2)])
    whole = lambda arr: pl.BlockSpec(arr.shape, lambda b, j: (0, 0))
    return pl.pallas_call(
        functools.partial(_diff_kernel, lam_init=lam_init),
        grid=(batch, nblk),
        in_specs=[pl.BlockSpec((MOBA_BLOCK, w), lambda b, j: (b * nblk + j, 0)),
                  pl.BlockSpec((seq, w), lambda b, j: (b, 0)),
                  pl.BlockSpec((seq, w), lambda b, j: (b, 0)),
                  whole(key_aux), whole(val_aux), whole(crow), whole(lams), whole(g)],
        out_specs=pl.BlockSpec((MOBA_BLOCK, w), lambda b, j: (b * nblk + j, 0)),
        out_shape=jax.ShapeDtypeStruct((batch * seq, w), BF16),
        scratch_shapes=_attend_scratch(2 * DIFF_HEADS, nblk),
        compiler_params=_params("arbitrary", "arbitrary"),
        name="diff",
    )(qd, kd, vd, key_aux, val_aux, crow, lams, g)


def _outproj_kernel(a_ref, d_ref, x_ref, wo_ref, g_ref, wq_ref, h_ref, q_ref, *, q_scale):
    wa = a_ref.shape[1]
    h = (x_ref[...]
         + jnp.dot(a_ref[...], wo_ref[:wa, :], preferred_element_type=F32)
         + jnp.dot(d_ref[...], wo_ref[wa:, :], preferred_element_type=F32))
    h_ref[...] = h
    qn = _rms(h, g_ref[...]).astype(BF16)
    q_ref[...] = (jnp.dot(qn, wq_ref[...], preferred_element_type=F32) * q_scale).astype(BF16)


def _outproj(a, d, x2, w_out, g, w_q, q_scale):
    t, dm = x2.shape
    tile = lambda w: pl.BlockSpec((TM_PROJ, w), lambda i: (i, 0))
    whole = lambda arr: pl.BlockSpec(arr.shape, lambda i: (0, 0))
    return pl.pallas_call(
        functools.partial(_outproj_kernel, q_scale=q_scale),
        grid=(t // TM_PROJ,),
        in_specs=[tile(a.shape[1]), tile(d.shape[1]), tile(dm), whole(w_out), whole(g), whole(w_q)],
        out_specs=[tile(dm), tile(dm)],
        out_shape=[jax.ShapeDtypeStruct((t, dm), F32), jax.ShapeDtypeStruct((t, dm), BF16)],
        compiler_params=_params("arbitrary"),
        name="outproj",
    )(a, d, x2, w_out, g, w_q)


def _memkv_kernel(m_ref, g_ref, w_ref, kv_ref):
    mn = _rms(m_ref[...], g_ref[...]).astype(BF16)
    kv_ref[...] = jnp.dot(mn, w_ref[...], preferred_element_type=F32).astype(BF16)


def _memkv(mem2, g, w_kv, mem_len):
    rows, dm = mem2.shape
    return pl.pallas_call(
        _memkv_kernel,
        grid=(rows // mem_len,),
        in_specs=[pl.BlockSpec((mem_len, dm), lambda i: (i, 0)),
                  pl.BlockSpec((1, dm), lambda i: (0, 0)),
                  pl.BlockSpec(w_kv.shape, lambda i: (0, 0))],
        out_specs=pl.BlockSpec((mem_len, w_kv.shape[1]), lambda i: (i, 0)),
        out_shape=jax.ShapeDtypeStruct((rows, w_kv.shape[1]), BF16),
        compiler_params=_params("arbitrary"),
        name="memkv",
    )(mem2, g, w_kv)


def _cross_kernel(q_ref, kv_ref, h_ref, wo_ref, g_ref, wr_ref, br_ref,
                  h2_ref, xn_ref, route_ref, counts_ref, run_ref):
    first = (pl.program_id(0) == 0) & (pl.program_id(1) == 0)

    @pl.when(first)
    def _():
        run_ref[...] = jnp.zeros_like(run_ref)

    tq, dm = h_ref.shape
    dh = dm // MEM_HEADS
    heads = []
    for h in range(MEM_HEADS):
        s = _dot_nt(q_ref[:, h * dh:(h + 1) * dh], kv_ref[:, h * dh:(h + 1) * dh])
        m = jnp.max(s, axis=-1, keepdims=True)
        e = jnp.exp(s - m)
        l = jnp.sum(e, axis=-1, keepdims=True)
        o = jnp.dot(e.astype(BF16), kv_ref[:, dm + h * dh:dm + (h + 1) * dh], preferred_element_type=F32)
        heads.append((o * (1.0 / l)).astype(BF16))
    o = jnp.concatenate(heads, axis=1)
    h2 = h_ref[...] + jnp.dot(o, wo_ref[...], preferred_element_type=F32)
    h2_ref[...] = h2
    xn = _rms(h2, g_ref[...])
    _to_token_major(xn_ref, xn)

    x_hi = xn.astype(BF16)
    x_lo = (xn - x_hi.astype(F32)).astype(BF16)
    hi = jnp.dot(x_hi, wr_ref[...], preferred_element_type=F32)
    lo_hi = jnp.dot(x_lo, wr_ref[:, :ROUTE_COLS], preferred_element_type=F32)
    logits = hi[:, :ROUTE_COLS] + (hi[:, ROUTE_COLS:] + lo_hi) + br_ref[...]
    lane = lax.broadcasted_iota(jnp.int32, logits.shape, 1).astype(F32)
    big = float(ROUTE_COLS)

    def top1(vals):
        v = jnp.max(vals, axis=-1, keepdims=True)
        i = jnp.min(jnp.where(vals == v, lane, big), axis=-1, keepdims=True)
        return v, i

    gl = jnp.where(lane < N_GROUPS, logits, -jnp.inf)
    g_max, g_idx = top1(gl)
    g_w = 1.0 / jnp.sum(jnp.exp(gl - g_max), axis=-1, keepdims=True)
    lo = EXPERT_COL0 + g_idx * EXPERTS_PER_GROUP
    el = jnp.where((lane >= lo) & (lane < lo + EXPERTS_PER_GROUP), logits, -jnp.inf)
    v1, i1 = top1(el)
    v2, i2 = top1(jnp.where(lane == i1, -jnp.inf, el))
    e2 = jnp.exp(v2 - v1)
    w1 = g_w * (1.0 / (1.0 + e2))
    w2 = g_w * (e2 / (1.0 + e2))

    onehot = jnp.where((lane == i1) | (lane == i2), 1.0, 0.0)
    run_ref[...] = run_ref[...] + jnp.sum(onehot, axis=0, keepdims=True)
    counts_ref[...] = run_ref[...]

    rec = jnp.zeros(logits.shape, F32)
    for col, val in enumerate((i1 - EXPERT_COL0, i2 - EXPERT_COL0, w1, w2)):
        rec = jnp.where(lane == col, val, rec)
    route_ref[...] = rec


def _cross(qc, kv, h1, w_o, g, w_r, b_r, batch, seq, mem_len):
    t, dm = h1.shape
    nt = seq // TQ_CROSS
    tile = lambda w: pl.BlockSpec((TQ_CROSS, w), lambda b, i: (b * nt + i, 0))
    whole = lambda arr: pl.BlockSpec(arr.shape, lambda b, i: (0, 0))
    return pl.pallas_call(
        _cross_kernel,
        grid=(batch, nt),
        in_specs=[tile(dm), pl.BlockSpec((mem_len, kv.shape[1]), lambda b, i: (b, 0)), tile(dm),
                  whole(w_o), whole(g), whole(w_r), whole(b_r)],
        out_specs=[tile(dm), pl.BlockSpec((TQ_CROSS * SUB, LANES), lambda b, i: (b * nt + i, 0)),
                   tile(ROUTE_COLS), pl.BlockSpec((1, ROUTE_COLS), lambda b, i: (0, 0))],
        out_shape=[jax.ShapeDtypeStruct((t, dm), F32), jax.ShapeDtypeStruct((t * SUB, LANES), F32),
                   jax.ShapeDtypeStruct((t, ROUTE_COLS), F32), jax.ShapeDtypeStruct((1, ROUTE_COLS), F32)],
        scratch_shapes=[pltpu.VMEM((1, ROUTE_COLS), F32)],
        compiler_params=_params("arbitrary", "arbitrary"),
        name="cross",
    )(qc, kv, h1, w_o, g, w_r, b_r)


SUB = 8


def _to_token_major(ref, x):
    rows = x.shape[0]
    for c in range(x.shape[1] // LANES):
        ref[pl.ds(c, rows, stride=SUB), :] = x[:, c * LANES:(c + 1) * LANES]


def _from_token_major(ref, rows):
    n_chunks = ref.shape[0] // rows
    return jnp.concatenate([ref[pl.ds(c, rows, stride=SUB), :] for c in range(n_chunks)], axis=1)


def _experts_kernel(te_ref, nused_ref, cs_ref, nv_ref, rows_ref, xn_ref, wg_ref, wu_ref, wd_ref, yo_ref,
                    xbuf0, xbuf1, ybuf0, ybuf1, wgu_bf, wd_bf, gsem, ssem, *, n_tok):
    i = pl.program_id(0)
    last = pl.num_programs(0) - 1
    n_used = nused_ref[0]
    xbuf, ybuf = (xbuf0, xbuf1), (ybuf0, ybuf1)
    tm = xbuf0.shape[0] // SUB
    ff = wd_ref.shape[0]
    token_rows = lambda tok: pl.ds(pl.multiple_of(tok * SUB, SUB), SUB)

    def gather(tile, buf):
        base = cs_ref[tile]
        for s in range(tm):
            row = rows_ref[jnp.minimum(base + s, 2 * n_tok - 1)]
            pltpu.make_async_copy(xn_ref.at[token_rows(row & (n_tok - 1))], xbuf[buf].at[token_rows(s)],
                                  gsem.at[buf]).start()

    def scatter(tile, buf):
        base = cs_ref[tile]

        def copy(s):
            return pltpu.make_async_copy(ybuf[buf].at[token_rows(s)], yo_ref.at[token_rows(rows_ref[base + s])],
                                         ssem.at[buf])

        @pl.when(nv_ref[tile] == tm)
        def _():
            for s in range(tm):
                copy(s).start()

        @pl.when(nv_ref[tile] < tm)
        def _():
            def body(s, c):
                copy(s).start()
                return c
            lax.fori_loop(0, nv_ref[tile], body, 0)

    @pl.when((i < n_used) & ((i == 0) | (te_ref[i] != te_ref[jnp.maximum(i - 1, 0)])))
    def _():
        wgu_bf[:, :ff] = wg_ref[...].astype(BF16)
        wgu_bf[:, ff:] = wu_ref[...].astype(BF16)
        wd_bf[...] = wd_ref[...].astype(BF16)

    def wait_gather(buf):
        pltpu.make_async_copy(xn_ref.at[pl.ds(0, tm * SUB)], xbuf[buf], gsem.at[buf]).wait()

    def wait_scatter(tile, buf):
        @pl.when(nv_ref[tile] == tm)
        def _():
            pltpu.make_async_copy(ybuf[buf], yo_ref.at[pl.ds(0, tm * SUB)], ssem.at[buf]).wait()

        @pl.when(nv_ref[tile] < tm)
        def _():
            def body(s, c):
                pltpu.make_async_copy(ybuf[buf].at[pl.ds(0, SUB)], yo_ref.at[pl.ds(0, SUB)], ssem.at[buf]).wait()
                return c
            lax.fori_loop(0, nv_ref[tile], body, 0)

    def step(cur):
        nxt = 1 - cur

        @pl.when((i == 0) & (n_used > 0))
        def _():
            gather(0, cur)

        @pl.when((i >= 2) & (i - 2 < n_used))
        def _():
            wait_scatter(i - 2, cur)

        @pl.when(i < n_used)
        def _():
            wait_gather(cur)
            gather(jnp.minimum(i + 1, n_used - 1), nxt)
            x = _from_token_major(xbuf[cur], tm).astype(BF16)
            gu = jnp.dot(x, wgu_bf[...], preferred_element_type=F32)
            g = gu[:, :ff]
            hh = (g * jax.nn.sigmoid(g)) * gu[:, ff:]
            _to_token_major(ybuf[cur], jnp.dot(hh.astype(BF16), wd_bf[...], preferred_element_type=F32))
            scatter(i, cur)

            @pl.when(i + 1 == n_used)
            def _():
                wait_gather(nxt)

        @pl.when(i == last)
        def _():
            @pl.when((i >= 1) & (i - 1 < n_used))
            def _():
                wait_scatter(i - 1, nxt)

            @pl.when(i < n_used)
            def _():
                wait_scatter(i, cur)

    for parity in range(2):
        pl.when(i % 2 == parity)(functools.partial(step, parity))


def _experts(tile_expert, n_used, tile_start, tile_valid, sorted_rows, xn_tm, w_gate, w_up, w_down, tm):
    n_tok = xn_tm.shape[0] // SUB
    n_tiles = tile_expert.shape[0]
    _, dm, ff = w_gate.shape
    assert n_tok & (n_tok - 1) == 0, "source token = output row & (n_tok - 1)"
    assert dm == SUB * LANES
    w_map = lambda i, te, nu, cs, nv, sr: (te[i], 0, 0)
    return pl.pallas_call(
        functools.partial(_experts_kernel, n_tok=n_tok),
        grid_spec=pltpu.PrefetchScalarGridSpec(
            num_scalar_prefetch=5,
            grid=(n_tiles,),
            in_specs=[pl.BlockSpec(memory_space=pl.ANY),
                      pl.BlockSpec((None, dm, ff), w_map), pl.BlockSpec((None, dm, ff), w_map),
                      pl.BlockSpec((None, ff, dm), w_map)],
            out_specs=pl.BlockSpec(memory_space=pl.ANY),
            scratch_shapes=[pltpu.VMEM((tm * SUB, LANES), F32)] * 4
            + [pltpu.VMEM((dm, 2 * ff), BF16), pltpu.VMEM((ff, dm), BF16),
                            pltpu.SemaphoreType.DMA((2,)), pltpu.SemaphoreType.DMA((2,))]),
        out_shape=jax.ShapeDtypeStruct((2 * n_tok * SUB, LANES), F32),
        compiler_params=_params("arbitrary"),
        name="experts",
    )(tile_expert, n_used, tile_start, tile_valid, sorted_rows, xn_tm, w_gate, w_up, w_down)


def _combine_kernel(h2_ref, route_ref, y0_ref, y1_ref, g_ref, o_ref):
    rows = h2_ref.shape[0]
    rec = route_ref[...]
    h = (h2_ref[...] + rec[:, 2:3] * _from_token_major(y0_ref, rows)
         + rec[:, 3:4] * _from_token_major(y1_ref, rows))
    o_ref[...] = _rms(h, g_ref[...])


def _combine(h2, route, yo, g):
    t, dm = h2.shape
    nt = t // TM_PROJ
    tile = lambda w: pl.BlockSpec((TM_PROJ, w), lambda i: (i, 0))
    return pl.pallas_call(
        _combine_kernel,
        grid=(nt,),
        in_specs=[tile(dm), tile(ROUTE_COLS),
                  pl.BlockSpec((TM_PROJ * SUB, LANES), lambda i: (i, 0)),
                  pl.BlockSpec((TM_PROJ * SUB, LANES), lambda i: (i + nt, 0)),
                  pl.BlockSpec((1, dm), lambda i: (0, 0))],
        out_specs=tile(dm),
        out_shape=jax.ShapeDtypeStruct((t, dm), F32),
        compiler_params=_params("arbitrary"),
        name="combine",
    )(h2, route, yo, yo, g)


def kernel(x, mem, norm_mix, w_in, lambda_q1, lambda_k1, lambda_q2, lambda_k2, diff_subln, norm_moba_out, w_out, norm_mem_q, norm_mem_kv, w_mem_q, w_mem_kv, w_mem_o, norm_ffn, w_router_group, b_router_group, w_router_expert, b_router_expert, w_expert_gate, w_expert_up, w_expert_down, norm_final):
    batch, seq, dm = x.shape
    mem_len = mem.shape[1]
    depth = w_in.shape[0]
    t = batch * seq
    assert seq % MOBA_BLOCK == 0 and seq % TQ_CROSS == 0 and t % TM_PROJ == 0

    h = x.reshape(t, dm)
    mem2 = mem.reshape(batch * mem_len, dm)
    row = lambda v: v.reshape(1, -1).astype(F32)
    for l in range(depth):
        lam_init = 0.8 - 0.6 * math.exp(-0.3 * l)
        q_scale = HEAD_DIM ** -0.5 * LOG2E
        qa, ka, va, qd, kd, vd = _inproj(h, row(norm_mix[l]), w_in[l].astype(BF16),
                                         (q_scale, 1.0, 1.0, q_scale, 1.0, 1.0))
        key_aux, val_aux = _key_aux(seq), _value_aux()
        a = _moba(qa, ka, va, key_aux, val_aux, row(norm_moba_out[l]), batch, seq)
        lams = jnp.stack([lambda_q1[l], lambda_k1[l], lambda_q2[l], lambda_k2[l]]).astype(F32)
        d = _diff(qd, kd, vd, key_aux, val_aux, lams, row(diff_subln[l]), batch, seq, lam_init)
        h1, qc = _outproj(a, d, h, w_out[l].astype(BF16), row(norm_mem_q[l]), w_mem_q[l].astype(BF16),
                          (dm // MEM_HEADS) ** -0.5)
        kv = _memkv(mem2, row(norm_mem_kv[l]), w_mem_kv[l].astype(BF16), mem_len)

        pad = ROUTE_COLS - N_GROUPS - N_EXPERTS
        w_r = jnp.pad(jnp.concatenate([w_router_group[l], w_router_expert[l]], axis=1).astype(F32), ((0, 0), (0, pad)))
        w_r_hi = w_r.astype(BF16)
        w_r = jnp.concatenate([w_r_hi, (w_r - w_r_hi.astype(F32)).astype(BF16)], axis=1)
        b_r = jnp.pad(jnp.concatenate([b_router_group[l], b_router_expert[l]]).astype(F32), (0, pad)).reshape(1, -1)
        h2, xn, route, counts = _cross(qc, kv, h1, w_mem_o[l].astype(BF16), row(norm_ffn[l]), w_r, b_r,
                                       batch, seq, mem_len)

        tm = TM_EXPERT
        n_tiles = (2 * t) // tm + N_EXPERTS
        cnt = counts[0, EXPERT_COL0:EXPERT_COL0 + N_EXPERTS].astype(jnp.int32)
        tiles_of = (cnt + tm - 1) // tm
        tile_end = jnp.cumsum(tiles_of)
        tile_idx = jnp.arange(n_tiles, dtype=jnp.int32)
        tile_expert = jnp.minimum(jnp.sum((tile_end[None, :] <= tile_idx[:, None]).astype(jnp.int32), axis=1),
                                  N_EXPERTS - 1)
        n_used = tile_end[-1:].astype(jnp.int32)
        onehot_te = (tile_expert[:, None] == jnp.arange(N_EXPERTS, dtype=jnp.int32)[None, :]).astype(jnp.int32)
        pick = lambda table: jnp.sum(onehot_te * table[None, :], axis=1)
        in_expert = (tile_idx - pick(tile_end - tiles_of)) * tm
        tile_start = pick(jnp.cumsum(cnt) - cnt) + in_expert
        tile_valid = jnp.where(tile_idx < n_used[0], jnp.clip(pick(cnt) - in_expert, 0, tm), 0).astype(jnp.int32)
        out_row = jnp.arange(t, dtype=jnp.int32)[:, None] + jnp.array([0, t], jnp.int32)[None, :]
        keys = route[:, 0:2].astype(jnp.int32) * (2 * t) + out_row
        sorted_rows = jnp.sort(keys.reshape(-1)) & (2 * t - 1)

        yo = _experts(tile_expert, n_used, tile_start.astype(jnp.int32), tile_valid, sorted_rows, xn,
                      w_expert_gate[l].astype(F32), w_expert_up[l].astype(F32), w_expert_down[l].astype(F32), tm)
        assert depth == 1
        h = _combine(h2, route, yo, row(norm_final))
    return h.reshape(batch, seq, dm)
```

```python
import functools
import math

import jax
import jax.numpy as jnp
import numpy as np
from jax import lax
from jax.experimental import pallas as pl
from jax.experimental.pallas import tpu as pltpu

F32 = jnp.float32
BF16 = jnp.bfloat16

HEAD_DIM = 64
MOBA_HEADS = 8
MOBA_WIDTH = MOBA_HEADS * HEAD_DIM
MOBA_BLOCK = 256
MOBA_TOPK = 3
DIFF_HEADS = 4
DIFF_V_DIM = 2 * HEAD_DIM
DIFF_WIDTH = DIFF_HEADS * DIFF_V_DIM
MEM_HEADS = 4
N_GROUPS = 4
EXPERTS_PER_GROUP = 8
N_EXPERTS = N_GROUPS * EXPERTS_PER_GROUP
RMS_EPS = 1e-6
NEG_INF = -1e30
LOG2E = math.log2(math.e)
LANES = 128
ROUTE_COLS = LANES
EXPERT_COL0 = N_GROUPS

TM_PROJ = 1024
TQ_ATTN = MOBA_BLOCK
TQ_CROSS = 512
TM_EXPERT = 256
VMEM_LIMIT = 56 * 1024 * 1024


def _alibi_slopes(n):
    return [2.0 ** (-8.0 * (i + 1) / n) for i in range(n)]


def _rms(x, g):
    y = x * lax.rsqrt(jnp.mean(x * x, axis=-1, keepdims=True) + RMS_EPS)
    return y * g


def _dot_nt(a, b):
    return lax.dot_general(a, b, (((1,), (1,)), ((), ())), preferred_element_type=F32)


def _params(*sem):
    return pltpu.CompilerParams(dimension_semantics=sem, vmem_limit_bytes=VMEM_LIMIT)


def _inproj_kernel(x_ref, g_ref, w_ref, *out_refs, scales):
    xn = _rms(x_ref[...], g_ref[...]).astype(BF16)
    width = out_refs[0].shape[1]
    for i, o_ref in enumerate(out_refs):
        y = jnp.dot(xn, w_ref[:, i * width:(i + 1) * width], preferred_element_type=F32)
        o_ref[...] = (y if scales[i] == 1.0 else y * scales[i]).astype(BF16)


def _inproj(x2, g, w, scales):
    t, d = x2.shape
    n_out = w.shape[1] // MOBA_WIDTH
    return pl.pallas_call(
        functools.partial(_inproj_kernel, scales=scales),
        grid=(t // TM_PROJ,),
        in_specs=[pl.BlockSpec((TM_PROJ, d), lambda i: (i, 0)),
                  pl.BlockSpec((1, d), lambda i: (0, 0)),
                  pl.BlockSpec(w.shape, lambda i: (0, 0))],
        out_specs=[pl.BlockSpec((TM_PROJ, MOBA_WIDTH), lambda i: (i, 0))] * n_out,
        out_shape=[jax.ShapeDtypeStruct((t, MOBA_WIDTH), BF16)] * n_out,
        compiler_params=_params("arbitrary"),
        name="inproj",
    )(x2, g, w)


AUX_PEN0 = 0
AUX_BLK0 = 64
AUX_OFF0 = 68
AUX_ROW = 72
N_SPLIT = 4


def _bf16_pieces(c):
    pieces, rem = [], np.float64(c)
    for _ in range(N_SPLIT):
        p = np.float64(np.asarray(rem).astype(BF16))
        pieces.append(p)
        rem = rem - p
    return pieces


def _key_aux(seq):
    pos = np.arange(seq)
    blk, off = pos // MOBA_BLOCK, pos % MOBA_BLOCK
    assert blk.max() < 8, "penalty lanes hold 8 blocks per head"
    a = np.zeros((seq, LANES), np.float32)
    for h in range(MOBA_HEADS):
        a[pos, AUX_PEN0 + h * 8 + blk] = 1.0
    a[:, AUX_BLK0:AUX_BLK0 + N_SPLIT] = (blk * MOBA_BLOCK)[:, None]
    a[:, AUX_OFF0:AUX_OFF0 + N_SPLIT] = off[:, None]
    a[:, AUX_ROW] = 1.0
    return jnp.asarray(a, BF16)


def _value_aux():
    a = np.zeros((4 * MOBA_BLOCK, LANES), np.float32)
    a[:, 0] = 1.0
    return jnp.asarray(a, BF16)


def _query_aux_rows(slopes):
    a = np.zeros((len(slopes), LANES), np.float32)
    for i, s in enumerate(slopes):
        a[i, AUX_BLK0:AUX_BLK0 + N_SPLIT] = _bf16_pieces(s * LOG2E)
        a[i, AUX_OFF0:AUX_OFF0 + N_SPLIT] = _bf16_pieces(s * LOG2E)
    return jnp.asarray(a)


def _query_aux(crow, slope, j, pen_t=None, pen_lo=0):
    blk = MOBA_BLOCK
    lane = lax.broadcasted_iota(jnp.int32, (blk, LANES), 1)
    t_q = (j * blk).astype(F32) + lax.broadcasted_iota(jnp.int32, (blk, LANES), 0).astype(F32)
    aux = jnp.where(lane == AUX_ROW, (-slope * LOG2E) * t_q, crow)
    if pen_t is not None:
        aux = jnp.where((lane >= pen_lo) & (lane < pen_lo + 8), pen_t, aux)
    return aux.astype(BF16)


def _attend(qaug_ref, k_ref, v_ref, kaux_ref, vaux_ref, s_ref, sown_ref, mrun_ref, acc_ref, j, kcols, vcols):
    n_items = qaug_ref.shape[0]
    blk = MOBA_BLOCK
    causal = (lax.broadcasted_iota(jnp.int32, (blk, blk), 0) >= lax.broadcasted_iota(jnp.int32, (blk, blk), 1))
    own = pl.ds(pl.multiple_of(j * blk, blk), blk)

    def scores(i, rows, kaux):
        return _dot_nt(qaug_ref[i], jnp.concatenate([k_ref[rows, kcols[i]], kaux], axis=1))

    def half_max(s):
        return jnp.maximum(s[:, :LANES], s[:, LANES:])

    kaux_own = kaux_ref[own, :]
    for i in range(n_items):
        s = jnp.where(causal, scores(i, own, kaux_own), NEG_INF)
        sown_ref[i] = s
        mrun_ref[i] = half_max(s)

    def pass_a(n, width):
        rows = pl.ds(pl.multiple_of(n * blk, blk), width * blk)
        kaux_n = kaux_ref[rows, :]
        for i in range(n_items):
            s = scores(i, rows, kaux_n)
            m = mrun_ref[i]
            for w in range(width):
                sw = s[:, w * blk:(w + 1) * blk]
                s_ref[i, n + w] = sw
                m = jnp.maximum(m, half_max(sw))
            mrun_ref[i] = m

    def loop_past(fn):
        def quad(t, c):
            fn(4 * t, 4)
            return c
        lax.fori_loop(0, j // 4, quad, 0)

        @pl.when((j // 2) % 2 == 1)
        def _():
            fn((j // 4) * 4, 2)

        @pl.when(j % 2 == 1)
        def _():
            fn(j - 1, 1)

    loop_past(pass_a)

    for i in range(n_items):
        mrun_ref[i] = jnp.broadcast_to(jnp.max(mrun_ref[i], axis=1, keepdims=True), (blk, LANES))

    def probs(i, s):
        m = mrun_ref[i]
        return jnp.concatenate([jnp.exp2(s[:, :LANES] - m), jnp.exp2(s[:, LANES:] - m)], axis=1).astype(BF16)

    def pv(i, p, rows, width):
        v_aug = jnp.concatenate([v_ref[rows, vcols[i]], vaux_ref[:width * blk, :]], axis=1)
        return jnp.dot(p, v_aug, preferred_element_type=F32)

    for i in range(n_items):
        acc_ref[i] = pv(i, probs(i, sown_ref[i]), own, 1)

    def pass_b(n, width):
        rows = pl.ds(pl.multiple_of(n * blk, blk), width * blk)
        for i in range(n_items):
            p = jnp.concatenate([probs(i, s_ref[i, n + w]) for w in range(width)], axis=1)
            acc_ref[i] += pv(i, p, rows, width)

    loop_past(pass_b)

    outs = []
    for i in range(n_items):
        a = acc_ref[i]
        outs.append(a[:, :LANES] * (1.0 / a[:, LANES:LANES + 1]))
    return outs


def _attend_scratch(n_items, nblk):
    blk = MOBA_BLOCK
    return [pltpu.VMEM((n_items, blk, 2 * LANES), BF16),
            pltpu.VMEM((n_items, nblk, blk, blk), F32),
            pltpu.VMEM((n_items, blk, blk), F32),
            pltpu.VMEM((n_items, blk, LANES), F32),
            pltpu.VMEM((n_items, blk, 2 * LANES), F32)]


def _moba_kernel(q_ref, k_ref, v_ref, kaux_ref, vaux_ref, crow_ref, g_ref, o_ref,
                 kmean_ref, kbd_ref, pen_ref, qaug_ref, s_ref, sown_ref, mrun_ref, acc_ref):
    j = pl.program_id(1)
    blk = MOBA_BLOCK
    nblk = s_ref.shape[1]
    slopes = _alibi_slopes(MOBA_HEADS)
    n_slabs = MOBA_WIDTH // LANES

    @pl.when(j == 0)
    def _():
        kmean_ref[...] = jnp.zeros_like(kmean_ref)
        for n in range(nblk):
            kb = k_ref[n * blk:(n + 1) * blk, :].astype(F32)
            kmean_ref[n:n + 1, :] = jnp.sum(kb, axis=0, keepdims=True) * (1.0 / blk)
        col_head = lax.broadcasted_iota(jnp.int32, kmean_ref.shape, 1) // HEAD_DIM
        for h in range(MOBA_HEADS):
            kbd_ref[h * 8:(h + 1) * 8, :] = jnp.where(col_head == h, kmean_ref[...], 0.0)

    pen_ref[...] = jnp.zeros_like(pen_ref)

    @pl.when(j > MOBA_TOPK)
    def _():
        kbd = kbd_ref[...]
        kbd_hi = kbd.astype(BF16)
        kbd_lo = (kbd - kbd_hi.astype(F32)).astype(BF16)
        gate = _dot_nt(kbd_hi, q_ref[...]) + _dot_nt(kbd_lo, q_ref[...])
        n_iota = lax.broadcasted_iota(jnp.int32, (8, blk), 0)
        for h in range(MOBA_HEADS):
            g = gate[h * 8:(h + 1) * 8, :]
            rank = jnp.zeros((8, blk), jnp.int32)
            for m in range(nblk):
                gm = g[m:m + 1, :]
                ahead = (gm > g) | ((gm == g) & (m < n_iota))
                rank = rank + jnp.where(ahead, 1, 0) * (m < j).astype(jnp.int32)
            pen_ref[h * 8:(h + 1) * 8, :] = jnp.where((n_iota < j) & (rank >= MOBA_TOPK), NEG_INF, 0.0)

    pen_t = pen_ref[...].T

    lane = lax.broadcasted_iota(jnp.int32, (blk, LANES), 1)
    for h in range(MOBA_HEADS):
        half = h % 2
        q_slab = q_ref[:, (h // 2) * LANES:(h // 2 + 1) * LANES]
        in_head = (lane >= half * HEAD_DIM) & (lane < (half + 1) * HEAD_DIM)
        aux = _query_aux(crow_ref[h:h + 1, :], slopes[h], j, pen_t, AUX_PEN0 + h * 8)
        qaug_ref[h] = jnp.concatenate([jnp.where(in_head, q_slab, 0).astype(BF16), aux], axis=1)

    cols = [slice((h // 2) * LANES, (h // 2 + 1) * LANES) for h in range(MOBA_HEADS)]
    outs = _attend(qaug_ref, k_ref, v_ref, kaux_ref, vaux_ref, s_ref, sown_ref, mrun_ref, acc_ref, j, cols, cols)
    o = jnp.concatenate([jnp.where(lane < HEAD_DIM, outs[2 * p], outs[2 * p + 1]) for p in range(n_slabs)], axis=1)
    o_ref[...] = _rms(o, g_ref[...]).astype(BF16)


def _moba(qa, ka, va, key_aux, val_aux, g, batch, seq):
    nblk = seq // MOBA_BLOCK
    w = MOBA_WIDTH
    crow = _query_aux_rows(_alibi_slopes(MOBA_HEADS))
    whole = lambda arr: pl.BlockSpec(arr.shape, lambda b, j: (0, 0))
    return pl.pallas_call(
        _moba_kernel,
        grid=(batch, nblk),
        in_specs=[pl.BlockSpec((MOBA_BLOCK, w), lambda b, j: (b * nblk + j, 0)),
                  pl.BlockSpec((seq, w), lambda b, j: (b, 0)),
                  pl.BlockSpec((seq, w), lambda b, j: (b, 0)),
                  whole(key_aux), whole(val_aux), whole(crow), whole(g)],
        out_specs=pl.BlockSpec((MOBA_BLOCK, w), lambda b, j: (b * nblk + j, 0)),
        out_shape=jax.ShapeDtypeStruct((batch * seq, w), BF16),
        scratch_shapes=[pltpu.VMEM((8, w), F32), pltpu.VMEM((8 * MOBA_HEADS, w), F32),
                        pltpu.VMEM((LANES, MOBA_BLOCK), F32)]
        + _attend_scratch(MOBA_HEADS, nblk),
        compiler_params=_params("arbitrary", "arbitrary"),
        name="moba",
    )(qa, ka, va, key_aux, val_aux, crow, g)


def _diff_kernel(q_ref, k_ref, v_ref, kaux_ref, vaux_ref, crow_ref, lam_ref, g_ref, o_ref,
                 qaug_ref, s_ref, sown_ref, mrun_ref, acc_ref, *, lam_init):
    j = pl.program_id(1)
    blk = MOBA_BLOCK
    slopes = _alibi_slopes(DIFF_HEADS)
    lane = lax.broadcasted_iota(jnp.int32, (blk, LANES), 1)

    lv = lam_ref[...]
    lam = (jnp.exp(jnp.sum(lv[0:1] * lv[1:2], axis=1, keepdims=True))
           - jnp.exp(jnp.sum(lv[2:3] * lv[3:4], axis=1, keepdims=True)) + lam_init)

    for i in range(2 * DIFF_HEADS):
        h, c = i // 2, i % 2
        q_slab = q_ref[:, h * LANES:(h + 1) * LANES]
        in_map = (lane >= c * HEAD_DIM) & (lane < (c + 1) * HEAD_DIM)
        aux = _query_aux(crow_ref[i:i + 1, :], slopes[h], j)
        qaug_ref[i] = jnp.concatenate([jnp.where(in_map, q_slab, 0).astype(BF16), aux], axis=1)

    cols = [slice((i // 2) * LANES, (i // 2 + 1) * LANES) for i in range(2 * DIFF_HEADS)]
    outs = _attend(qaug_ref, k_ref, v_ref, kaux_ref, vaux_ref, s_ref, sown_ref, mrun_ref, acc_ref, j, cols, cols)
    for h in range(DIFF_HEADS):
        o = outs[2 * h] - lam * outs[2 * h + 1]
        o_ref[:, h * LANES:(h + 1) * LANES] = (_rms(o, g_ref[...]) * (1.0 - lam_init)).astype(BF16)


def _diff(qd, kd, vd, key_aux, val_aux, lams, g, batch, seq, lam_init):
    nblk = seq // MOBA_BLOCK
    w = DIFF_WIDTH
    crow = _query_aux_rows([s for s in _alibi_slopes(DIFF_HEADS) for _ in range(2)])
    whole = lambda arr: pl.BlockSpec(arr.shape, lambda b, j: (0, 0))
    return pl.pallas_call(
        functools.partial(_diff_kernel, lam_init=lam_init),
        grid=(batch, nblk),
        in_specs=[pl.BlockSpec((MOBA_BLOCK, w), lambda b, j: (b * nblk + j, 0)),
                  pl.BlockSpec((seq, w), lambda b, j: (b, 0)),
                  pl.BlockSpec((seq, w), lambda b, j: (b, 0)),
                  whole(key_aux), whole(val_aux), whole(crow), whole(lams), whole(g)],
        out_specs=pl.BlockSpec((MOBA_BLOCK, w), lambda b, j: (b * nblk + j, 0)),
        out_shape=jax.ShapeDtypeStruct((batch * seq, w), BF16),
        scratch_shapes=_attend_scratch(2 * DIFF_HEADS, nblk),
        compiler_params=_params("arbitrary", "arbitrary"),
        name="diff",
    )(qd, kd, vd, key_aux, val_aux, crow, lams, g)


def _outproj_kernel(a_ref, d_ref, x_ref, wo_ref, g_ref, wq_ref, h_ref, q_ref, *, q_scale):
    wa = a_ref.shape[1]
    h = (x_ref[...]
         + jnp.dot(a_ref[...], wo_ref[:wa, :], preferred_element_type=F32)
         + jnp.dot(d_ref[...], wo_ref[wa:, :], preferred_element_type=F32))
    h_ref[...] = h
    qn = _rms(h, g_ref[...]).astype(BF16)
    q_ref[...] = (jnp.dot(qn, wq_ref[...], preferred_element_type=F32) * q_scale).astype(BF16)


def _outproj(a, d, x2, w_out, g, w_q, q_scale):
    t, dm = x2.shape
    tile = lambda w: pl.BlockSpec((TM_PROJ, w), lambda i: (i, 0))
    whole = lambda arr: pl.BlockSpec(arr.shape, lambda i: (0, 0))
    return pl.pallas_call(
        functools.partial(_outproj_kernel, q_scale=q_scale),
        grid=(t // TM_PROJ,),
        in_specs=[tile(a.shape[1]), tile(d.shape[1]), tile(dm), whole(w_out), whole(g), whole(w_q)],
        out_specs=[tile(dm), tile(dm)],
        out_shape=[jax.ShapeDtypeStruct((t, dm), F32), jax.ShapeDtypeStruct((t, dm), BF16)],
        compiler_params=_params("arbitrary"),
        name="outproj",
    )(a, d, x2, w_out, g, w_q)


def _memkv_kernel(m_ref, g_ref, w_ref, kv_ref):
    mn = _rms(m_ref[...], g_ref[...]).astype(BF16)
    kv_ref[...] = jnp.dot(mn, w_ref[...], preferred_element_type=F32).astype(BF16)


def _memkv(mem2, g, w_kv, mem_len):
    rows, dm = mem2.shape
    return pl.pallas_call(
        _memkv_kernel,
        grid=(rows // mem_len,),
        in_specs=[pl.BlockSpec((mem_len, dm), lambda i: (i, 0)),
                  pl.BlockSpec((1, dm), lambda i: (0, 0)),
                  pl.BlockSpec(w_kv.shape, lambda i: (0, 0))],
        out_specs=pl.BlockSpec((mem_len, w_kv.shape[1]), lambda i: (i, 0)),
        out_shape=jax.ShapeDtypeStruct((rows, w_kv.shape[1]), BF16),
        compiler_params=_params("arbitrary"),
        name="memkv",
    )(mem2, g, w_kv)


def _cross_kernel(q_ref, kv_ref, h_ref, wo_ref, g_ref, wr_ref, br_ref,
                  h2_ref, xn_ref, route_ref, counts_ref, run_ref):
    first = (pl.program_id(0) == 0) & (pl.program_id(1) == 0)

    @pl.when(first)
    def _():
        run_ref[...] = jnp.zeros_like(run_ref)

    tq, dm = h_ref.shape
    dh = dm // MEM_HEADS
    heads = []
    for h in range(MEM_HEADS):
        s = _dot_nt(q_ref[:, h * dh:(h + 1) * dh], kv_ref[:, h * dh:(h + 1) * dh])
        m = jnp.max(s, axis=-1, keepdims=True)
        e = jnp.exp(s - m)
        l = jnp.sum(e, axis=-1, keepdims=True)
        o = jnp.dot(e.astype(BF16), kv_ref[:, dm + h * dh:dm + (h + 1) * dh], preferred_element_type=F32)
        heads.append((o * (1.0 / l)).astype(BF16))
    o = jnp.concatenate(heads, axis=1)
    h2 = h_ref[...] + jnp.dot(o, wo_ref[...], preferred_element_type=F32)
    h2_ref[...] = h2
    xn = _rms(h2, g_ref[...])
    _to_token_major(xn_ref, xn)

    x_hi = xn.astype(BF16)
    x_lo = (xn - x_hi.astype(F32)).astype(BF16)
    hi = jnp.dot(x_hi, wr_ref[...], preferred_element_type=F32)
    lo_hi = jnp.dot(x_lo, wr_ref[:, :ROUTE_COLS], preferred_element_type=F32)
    logits = hi[:, :ROUTE_COLS] + (hi[:, ROUTE_COLS:] + lo_hi) + br_ref[...]
    lane = lax.broadcasted_iota(jnp.int32, logits.shape, 1).astype(F32)
    big = float(ROUTE_COLS)

    def top1(vals):
        v = jnp.max(vals, axis=-1, keepdims=True)
        i = jnp.min(jnp.where(vals == v, lane, big), axis=-1, keepdims=True)
        return v, i

    gl = jnp.where(lane < N_GROUPS, logits, -jnp.inf)
    g_max, g_idx = top1(gl)
    g_w = 1.0 / jnp.sum(jnp.exp(gl - g_max), axis=-1, keepdims=True)
    lo = EXPERT_COL0 + g_idx * EXPERTS_PER_GROUP
    el = jnp.where((lane >= lo) & (lane < lo + EXPERTS_PER_GROUP), logits, -jnp.inf)
    v1, i1 = top1(el)
    v2, i2 = top1(jnp.where(lane == i1, -jnp.inf, el))
    e2 = jnp.exp(v2 - v1)
    w1 = g_w * (1.0 / (1.0 + e2))
    w2 = g_w * (e2 / (1.0 + e2))

    onehot = jnp.where((lane == i1) | (lane == i2), 1.0, 0.0)
    run_ref[...] = run_ref[...] + jnp.sum(onehot, axis=0, keepdims=True)
    counts_ref[...] = run_ref[...]

    rec = jnp.zeros(logits.shape, F32)
    for col, val in enumerate((i1 - EXPERT_COL0, i2 - EXPERT_COL0, w1, w2)):
        rec = jnp.where(lane == col, val, rec)
    route_ref[...] = rec


def _cross(qc, kv, h1, w_o, g, w_r, b_r, batch, seq, mem_len):
    t, dm = h1.shape
    nt = seq // TQ_CROSS
    tile = lambda w: pl.BlockSpec((TQ_CROSS, w), lambda b, i: (b * nt + i, 0))
    whole = lambda arr: pl.BlockSpec(arr.shape, lambda b, i: (0, 0))
    return pl.pallas_call(
        _cross_kernel,
        grid=(batch, nt),
        in_specs=[tile(dm), pl.BlockSpec((mem_len, kv.shape[1]), lambda b, i: (b, 0)), tile(dm),
                  whole(w_o), whole(g), whole(w_r), whole(b_r)],
        out_specs=[tile(dm), pl.BlockSpec((TQ_CROSS * SUB, LANES), lambda b, i: (b * nt + i, 0)),
                   tile(ROUTE_COLS), pl.BlockSpec((1, ROUTE_COLS), lambda b, i: (0, 0))],
        out_shape=[jax.ShapeDtypeStruct((t, dm), F32), jax.ShapeDtypeStruct((t * SUB, LANES), F32),
                   jax.ShapeDtypeStruct((t, ROUTE_COLS), F32), jax.ShapeDtypeStruct((1, ROUTE_COLS), F32)],
        scratch_shapes=[pltpu.VMEM((1, ROUTE_COLS), F32)],
        compiler_params=_params("arbitrary", "arbitrary"),
        name="cross",
    )(qc, kv, h1, w_o, g, w_r, b_r)


SUB = 8


def _to_token_major(ref, x):
    rows = x.shape[0]
    for c in range(x.shape[1] // LANES):
        ref[pl.ds(c, rows, stride=SUB), :] = x[:, c * LANES:(c + 1) * LANES]


def _from_token_major(ref, rows):
    n_chunks = ref.shape[0] // rows
    return jnp.concatenate([ref[pl.ds(c, rows, stride=SUB), :] for c in range(n_chunks)], axis=1)


def _experts_kernel(te_ref, nused_ref, cs_ref, nv_ref, rows_ref, xn_ref, wg_ref, wu_ref, wd_ref, yo_ref,
                    xbuf0, xbuf1, ybuf0, ybuf1, wgu_bf, wd_bf, gsem, ssem, *, n_tok):
    i = pl.program_id(0)
    last = pl.num_programs(0) - 1
    n_used = nused_ref[0]
    xbuf, ybuf = (xbuf0, xbuf1), (ybuf0, ybuf1)
    tm = xbuf0.shape[0] // SUB
    ff = wd_ref.shape[0]
    token_rows = lambda tok: pl.ds(pl.multiple_of(tok * SUB, SUB), SUB)

    def gather(tile, buf):
        base = cs_ref[tile]
        for s in range(tm):
            row = rows_ref[jnp.minimum(base + s, 2 * n_tok - 1)]
            pltpu.make_async_copy(xn_ref.at[token_rows(row & (n_tok - 1))], xbuf[buf].at[token_rows(s)],
                                  gsem.at[buf]).start()

    def scatter(tile, buf):
        base = cs_ref[tile]

        def copy(s):
            return pltpu.make_async_copy(ybuf[buf].at[token_rows(s)], yo_ref.at[token_rows(rows_ref[base + s])],
                                         ssem.at[buf])

        @pl.when(nv_ref[tile] == tm)
        def _():
            for s in range(tm):
                copy(s).start()

        @pl.when(nv_ref[tile] < tm)
        def _():
            def body(s, c):
                copy(s).start()
                return c
            lax.fori_loop(0, nv_ref[tile], body, 0)

    @pl.when((i < n_used) & ((i == 0) | (te_ref[i] != te_ref[jnp.maximum(i - 1, 0)])))
    def _():
        wgu_bf[:, :ff] = wg_ref[...].astype(BF16)
        wgu_bf[:, ff:] = wu_ref[...].astype(BF16)
        wd_bf[...] = wd_ref[...].astype(BF16)

    def wait_gather(buf):
        pltpu.make_async_copy(xn_ref.at[pl.ds(0, tm * SUB)], xbuf[buf], gsem.at[buf]).wait()

    def wait_scatter(tile, buf):
        @pl.when(nv_ref[tile] == tm)
        def _():
            pltpu.make_async_copy(ybuf[buf], yo_ref.at[pl.ds(0, tm * SUB)], ssem.at[buf]).wait()

        @pl.when(nv_ref[tile] < tm)
        def _():
            def body(s, c):
                pltpu.make_async_copy(ybuf[buf].at[pl.ds(0, SUB)], yo_ref.at[pl.ds(0, SUB)], ssem.at[buf]).wait()
                return c
            lax.fori_loop(0, nv_ref[tile], body, 0)

    def step(cur):
        nxt = 1 - cur

        @pl.when((i == 0) & (n_used > 0))
        def _():
            gather(0, cur)

        @pl.when((i >= 2) & (i - 2 < n_used))
        def _():
            wait_scatter(i - 2, cur)

        @pl.when(i < n_used)
        def _():
            wait_gather(cur)
            gather(jnp.minimum(i + 1, n_used - 1), nxt)
            x = _from_token_major(xbuf[cur], tm).astype(BF16)
            gu = jnp.dot(x, wgu_bf[...], preferred_element_type=F32)
            g = gu[:, :ff]
            hh = (g * jax.nn.sigmoid(g)) * gu[:, ff:]
            _to_token_major(ybuf[cur], jnp.dot(hh.astype(BF16), wd_bf[...], preferred_element_type=F32))
            scatter(i, cur)

            @pl.when(i + 1 == n_used)
            def _():
                wait_gather(nxt)

        @pl.when(i == last)
        def _():
            @pl.when((i >= 1) & (i - 1 < n_used))
            def _():
                wait_scatter(i - 1, nxt)

            @pl.when(i < n_used)
            def _():
                wait_scatter(i, cur)

    for parity in range(2):
        pl.when(i % 2 == parity)(functools.partial(step, parity))


def _experts(tile_expert, n_used, tile_start, tile_valid, sorted_rows, xn_tm, w_gate, w_up, w_down, tm):
    n_tok = xn_tm.shape[0] // SUB
    n_tiles = tile_expert.shape[0]
    _, dm, ff = w_gate.shape
    assert n_tok & (n_tok - 1) == 0, "source token = output row & (n_tok - 1)"
    assert dm == SUB * LANES
    w_map = lambda i, te, nu, cs, nv, sr: (te[i], 0, 0)
    return pl.pallas_call(
        functools.partial(_experts_kernel, n_tok=n_tok),
        grid_spec=pltpu.PrefetchScalarGridSpec(
            num_scalar_prefetch=5,
            grid=(n_tiles,),
            in_specs=[pl.BlockSpec(memory_space=pl.ANY),
                      pl.BlockSpec((None, dm, ff), w_map), pl.BlockSpec((None, dm, ff), w_map),
                      pl.BlockSpec((None, ff, dm), w_map)],
            out_specs=pl.BlockSpec(memory_space=pl.ANY),
            scratch_shapes=[pltpu.VMEM((tm * SUB, LANES), F32)] * 4
            + [pltpu.VMEM((dm, 2 * ff), BF16), pltpu.VMEM((ff, dm), BF16),
                            pltpu.SemaphoreType.DMA((2,)), pltpu.SemaphoreType.DMA((2,))]),
        out_shape=jax.ShapeDtypeStruct((2 * n_tok * SUB, LANES), F32),
        compiler_params=_params("arbitrary"),
        name="experts",
    )(tile_expert, n_used, tile_start, tile_valid, sorted_rows, xn_tm, w_gate, w_up, w_down)


def _combine_kernel(h2_ref, route_ref, y0_ref, y1_ref, g_ref, o_ref):
    rows = h2_ref.shape[0]
    rec = route_ref[...]
    h = (h2_ref[...] + rec[:, 2:3] * _from_token_major(y0_ref, rows)
         + rec[:, 3:4] * _from_token_major(y1_ref, rows))
    o_ref[...] = _rms(h, g_ref[...])


def _combine(h2, route, yo, g):
    t, dm = h2.shape
    nt = t // TM_PROJ
    tile = lambda w: pl.BlockSpec((TM_PROJ, w), lambda i: (i, 0))
    return pl.pallas_call(
        _combine_kernel,
        grid=(nt,),
        in_specs=[tile(dm), tile(ROUTE_COLS),
                  pl.BlockSpec((TM_PROJ * SUB, LANES), lambda i: (i, 0)),
                  pl.BlockSpec((TM_PROJ * SUB, LANES), lambda i: (i + nt, 0)),
                  pl.BlockSpec((1, dm), lambda i: (0, 0))],
        out_specs=tile(dm),
        out_shape=jax.ShapeDtypeStruct((t, dm), F32),
        compiler_params=_params("arbitrary"),
        name="combine",
    )(h2, route, yo, yo, g)


def kernel(x, mem, norm_mix, w_in, lambda_q1, lambda_k1, lambda_q2, lambda_k2, diff_subln, norm_moba_out, w_out, norm_mem_q, norm_mem_kv, w_mem_q, w_mem_kv, w_mem_o, norm_ffn, w_router_group, b_router_group, w_router_expert, b_router_expert, w_expert_gate, w_expert_up, w_expert_down, norm_final):
    batch, seq, dm = x.shape
    mem_len = mem.shape[1]
    depth = w_in.shape[0]
    t = batch * seq
    assert seq % MOBA_BLOCK == 0 and seq % TQ_CROSS == 0 and t % TM_PROJ == 0

    h = x.reshape(t, dm)
    mem2 = mem.reshape(batch * mem_len, dm)
    row = lambda v: v.reshape(1, -1).astype(F32)
    for l in range(depth):
        lam_init = 0.8 - 0.6 * math.exp(-0.3 * l)
        q_scale = HEAD_DIM ** -0.5 * LOG2E
        qa, ka, va, qd, kd, vd = _inproj(h, row(norm_mix[l]), w_in[l].astype(BF16),
                                         (q_scale, 1.0, 1.0, q_scale, 1.0, 1.0))
        key_aux, val_aux = _key_aux(seq), _value_aux()
        a = _moba(qa, ka, va, key_aux, val_aux, row(norm_moba_out[l]), batch, seq)
        lams = jnp.stack([lambda_q1[l], lambda_k1[l], lambda_q2[l], lambda_k2[l]]).astype(F32)
        d = _diff(qd, kd, vd, key_aux, val_aux, lams, row(diff_subln[l]), batch, seq, lam_init)
        h1, qc = _outproj(a, d, h, w_out[l].astype(BF16), row(norm_mem_q[l]), w_mem_q[l].astype(BF16),
                          (dm // MEM_HEADS) ** -0.5)
        kv = _memkv(mem2, row(norm_mem_kv[l]), w_mem_kv[l].astype(BF16), mem_len)

        pad = ROUTE_COLS - N_GROUPS - N_EXPERTS
        w_r = jnp.pad(jnp.concatenate([w_router_group[l], w_router_expert[l]], axis=1).astype(F32), ((0, 0), (0, pad)))
        w_r_hi = w_r.astype(BF16)
        w_r = jnp.concatenate([w_r_hi, (w_r - w_r_hi.astype(F32)).astype(BF16)], axis=1)
        b_r = jnp.pad(jnp.concatenate([b_router_group[l], b_router_expert[l]]).astype(F32), (0, pad)).reshape(1, -1)
        h2, xn, route, counts = _cross(qc, kv, h1, w_mem_o[l].astype(BF16), row(norm_ffn[l]), w_r, b_r,
                                       batch, seq, mem_len)

        tm = TM_EXPERT
        n_tiles = (2 * t) // tm + N_EXPERTS
        cnt = counts[0, EXPERT_COL0:EXPERT_COL0 + N_EXPERTS].astype(jnp.int32)
        tiles_of = (cnt + tm - 1) // tm
        tile_end = jnp.cumsum(tiles_of)
        tile_idx = jnp.arange(n_tiles, dtype=jnp.int32)
        tile_expert = jnp.minimum(jnp.sum((tile_end[None, :] <= tile_idx[:, None]).astype(jnp.int32), axis=1),
                                  N_EXPERTS - 1)
        n_used = tile_end[-1:].astype(jnp.int32)
        onehot_te = (tile_expert[:, None] == jnp.arange(N_EXPERTS, dtype=jnp.int32)[None, :]).astype(jnp.int32)
        pick = lambda table: jnp.sum(onehot_te * table[None, :], axis=1)
        in_expert = (tile_idx - pick(tile_end - tiles_of)) * tm
        tile_start = pick(jnp.cumsum(cnt) - cnt) + in_expert
        tile_valid = jnp.where(tile_idx < n_used[0], jnp.clip(pick(cnt) - in_expert, 0, tm), 0).astype(jnp.int32)
        out_row = jnp.arange(t, dtype=jnp.int32)[:, None] + jnp.array([0, t], jnp.int32)[None, :]
        keys = route[:, 0:2].astype(jnp.int32) * (2 * t) + out_row
        sorted_rows = jnp.sort(keys.reshape(-1)) & (2 * t - 1)

        yo = _experts(tile_expert, n_used, tile_start.astype(jnp.int32), tile_valid, sorted_rows, xn,
                      w_expert_gate[l].astype(F32), w_expert_up[l].astype(F32), w_expert_down[l].astype(F32), tm)
        assert depth == 1
        h = _combine(h2, route, yo, row(norm_final))
    return h.reshape(batch, seq, dm)
```

```python
import functools
import math

import jax
import jax.numpy as jnp
import numpy as np
from jax import lax
from jax.experimental import pallas as pl
from jax.experimental.pallas import tpu as pltpu

F32 = jnp.float32
BF16 = jnp.bfloat16

HEAD_DIM = 64
MOBA_HEADS = 8
MOBA_WIDTH = MOBA_HEADS * HEAD_DIM
MOBA_BLOCK = 256
MOBA_TOPK = 3
DIFF_HEADS = 4
DIFF_V_DIM = 2 * HEAD_DIM
DIFF_WIDTH = DIFF_HEADS * DIFF_V_DIM
MEM_HEADS = 4
N_GROUPS = 4
EXPERTS_PER_GROUP = 8
N_EXPERTS = N_GROUPS * EXPERTS_PER_GROUP
RMS_EPS = 1e-6
NEG_INF = -1e30
LOG2E = math.log2(math.e)
LANES = 128
ROUTE_COLS = LANES
EXPERT_COL0 = N_GROUPS

TM_PROJ = 1024
TQ_ATTN = MOBA_BLOCK
TQ_CROSS = 512
TM_EXPERT = 256
VMEM_LIMIT = 56 * 1024 * 1024


def _alibi_slopes(n):
    return [2.0 ** (-8.0 * (i + 1) / n) for i in range(n)]


def _rms(x, g):
    y = x * lax.rsqrt(jnp.mean(x * x, axis=-1, keepdims=True) + RMS_EPS)
    return y * g


def _dot_nt(a, b):
    return lax.dot_general(a, b, (((1,), (1,)), ((), ())), preferred_element_type=F32)


def _params(*sem):
    return pltpu.CompilerParams(dimension_semantics=sem, vmem_limit_bytes=VMEM_LIMIT)


def _inproj_kernel(x_ref, g_ref, w_ref, *out_refs, scales):
    xn = _rms(x_ref[...], g_ref[...]).astype(BF16)
    width = out_refs[0].shape[1]
    for i, o_ref in enumerate(out_refs):
        y = jnp.dot(xn, w_ref[:, i * width:(i + 1) * width], preferred_element_type=F32)
        o_ref[...] = (y if scales[i] == 1.0 else y * scales[i]).astype(BF16)


def _inproj(x2, g, w, scales):
    t, d = x2.shape
    n_out = w.shape[1] // MOBA_WIDTH
    return pl.pallas_call(
        functools.partial(_inproj_kernel, scales=scales),
        grid=(t // TM_PROJ,),
        in_specs=[pl.BlockSpec((TM_PROJ, d), lambda i: (i, 0)),
                  pl.BlockSpec((1, d), lambda i: (0, 0)),
                  pl.BlockSpec(w.shape, lambda i: (0, 0))],
        out_specs=[pl.BlockSpec((TM_PROJ, MOBA_WIDTH), lambda i: (i, 0))] * n_out,
        out_shape=[jax.ShapeDtypeStruct((t, MOBA_WIDTH), BF16)] * n_out,
        compiler_params=_params("arbitrary"),
        name="inproj",
    )(x2, g, w)


AUX_PEN0 = 0
AUX_BLK0 = 64
AUX_OFF0 = 68
AUX_ROW = 72
N_SPLIT = 4


def _bf16_pieces(c):
    pieces, rem = [], np.float64(c)
    for _ in range(N_SPLIT):
        p = np.float64(np.asarray(rem).astype(BF16))
        pieces.append(p)
        rem = rem - p
    return pieces


def _key_aux(seq):
    pos = np.arange(seq)
    blk, off = pos // MOBA_BLOCK, pos % MOBA_BLOCK
    assert blk.max() < 8, "penalty lanes hold 8 blocks per head"
    a = np.zeros((seq, LANES), np.float32)
    for h in range(MOBA_HEADS):
        a[pos, AUX_PEN0 + h * 8 + blk] = 1.0
    a[:, AUX_BLK0:AUX_BLK0 + N_SPLIT] = (blk * MOBA_BLOCK)[:, None]
    a[:, AUX_OFF0:AUX_OFF0 + N_SPLIT] = off[:, None]
    a[:, AUX_ROW] = 1.0
    return jnp.asarray(a, BF16)


def _value_aux():
    a = np.zeros((4 * MOBA_BLOCK, LANES), np.float32)
    a[:, 0] = 1.0
    return jnp.asarray(a, BF16)


def _query_aux_rows(slopes):
    a = np.zeros((len(slopes), LANES), np.float32)
    for i, s in enumerate(slopes):
        a[i, AUX_BLK0:AUX_BLK0 + N_SPLIT] = _bf16_pieces(s * LOG2E)
        a[i, AUX_OFF0:AUX_OFF0 + N_SPLIT] = _bf16_pieces(s * LOG2E)
    return jnp.asarray(a)


def _query_aux(crow, slope, j, pen_t=None, pen_lo=0):
    blk = MOBA_BLOCK
    lane = lax.broadcasted_iota(jnp.int32, (blk, LANES), 1)
    t_q = (j * blk).astype(F32) + lax.broadcasted_iota(jnp.int32, (blk, LANES), 0).astype(F32)
    aux = jnp.where(lane == AUX_ROW, (-slope * LOG2E) * t_q, crow)
    if pen_t is not None:
        aux = jnp.where((lane >= pen_lo) & (lane < pen_lo + 8), pen_t, aux)
    return aux.astype(BF16)


def _attend(qaug_ref, k_ref, v_ref, kaux_ref, vaux_ref, s_ref, sown_ref, mrun_ref, acc_ref, j, kcols, vcols):
    n_items = qaug_ref.shape[0]
    blk = MOBA_BLOCK
    causal = (lax.broadcasted_iota(jnp.int32, (blk, blk), 0) >= lax.broadcasted_iota(jnp.int32, (blk, blk), 1))
    own = pl.ds(pl.multiple_of(j * blk, blk), blk)

    def scores(i, rows, kaux):
        return _dot_nt(qaug_ref[i], jnp.concatenate([k_ref[rows, kcols[i]], kaux], axis=1))

    def half_max(s):
        return jnp.maximum(s[:, :LANES], s[:, LANES:])

    kaux_own = kaux_ref[own, :]
    for i in range(n_items):
        s = jnp.where(causal, scores(i, own, kaux_own), NEG_INF)
        sown_ref[i] = s
        mrun_ref[i] = half_max(s)

    def pass_a(n, width):
        rows = pl.ds(pl.multiple_of(n * blk, blk), width * blk)
        kaux_n = kaux_ref[rows, :]
        for i in range(n_items):
            s = scores(i, rows, kaux_n)
            m = mrun_ref[i]
            for w in range(width):
                sw = s[:, w * blk:(w + 1) * blk]
                s_ref[i, n + w] = sw
                m = jnp.maximum(m, half_max(sw))
            mrun_ref[i] = m

    def loop_past(fn):
        def quad(t, c):
            fn(4 * t, 4)
            return c
        lax.fori_loop(0, j // 4, quad, 0)

        @pl.when((j // 2) % 2 == 1)
        def _():
            fn((j // 4) * 4, 2)

        @pl.when(j % 2 == 1)
        def _():
            fn(j - 1, 1)

    loop_past(pass_a)

    for i in range(n_items):
        mrun_ref[i] = jnp.broadcast_to(jnp.max(mrun_ref[i], axis=1, keepdims=True), (blk, LANES))

    def probs(i, s):
        m = mrun_ref[i]
        return jnp.concatenate([jnp.exp2(s[:, :LANES] - m), jnp.exp2(s[:, LANES:] - m)], axis=1).astype(BF16)

    def pv(i, p, rows, width):
        v_aug = jnp.concatenate([v_ref[rows, vcols[i]], vaux_ref[:width * blk, :]], axis=1)
        return jnp.dot(p, v_aug, preferred_element_type=F32)

    for i in range(n_items):
        acc_ref[i] = pv(i, probs(i, sown_ref[i]), own, 1)

    def pass_b(n, width):
        rows = pl.ds(pl.multiple_of(n * blk, blk), width * blk)
        for i in range(n_items):
            p = jnp.concatenate([probs(i, s_ref[i, n + w]) for w in range(width)], axis=1)
            acc_ref[i] += pv(i, p, rows, width)

    loop_past(pass_b)

    outs = []
    for i in range(n_items):
        a = acc_ref[i]
        outs.append(a[:, :LANES] * (1.0 / a[:, LANES:LANES + 1]))
    return outs


def _attend_scratch(n_items, nblk):
    blk = MOBA_BLOCK
    return [pltpu.VMEM((n_items, blk, 2 * LANES), BF16),
            pltpu.VMEM((n_items, nblk, blk, blk), F32),
            pltpu.VMEM((n_items, blk, blk), F32),
            pltpu.VMEM((n_items, blk, LANES), F32),
            pltpu.VMEM((n_items, blk, 2 * LANES), F32)]


def _moba_kernel(q_ref, k_ref, v_ref, kaux_ref, vaux_ref, crow_ref, g_ref, o_ref,
                 kmean_ref, kbd_ref, pen_ref, qaug_ref, s_ref, sown_ref, mrun_ref, acc_ref):
    j = pl.program_id(1)
    blk = MOBA_BLOCK
    nblk = s_ref.shape[1]
    slopes = _alibi_slopes(MOBA_HEADS)
    n_slabs = MOBA_WIDTH // LANES

    @pl.when(j == 0)
    def _():
        kmean_ref[...] = jnp.zeros_like(kmean_ref)
        for n in range(nblk):
            kb = k_ref[n * blk:(n + 1) * blk, :].astype(F32)
            kmean_ref[n:n + 1, :] = jnp.sum(kb, axis=0, keepdims=True) * (1.0 / blk)
        col_head = lax.broadcasted_iota(jnp.int32, kmean_ref.shape, 1) // HEAD_DIM
        for h in range(MOBA_HEADS):
            kbd_ref[h * 8:(h + 1) * 8, :] = jnp.where(col_head == h, kmean_ref[...], 0.0)

    pen_ref[...] = jnp.zeros_like(pen_ref)

    @pl.when(j > MOBA_TOPK)
    def _():
        kbd = kbd_ref[...]
        kbd_hi = kbd.astype(BF16)
        kbd_lo = (kbd - kbd_hi.astype(F32)).astype(BF16)
        gate = _dot_nt(kbd_hi, q_ref[...]) + _dot_nt(kbd_lo, q_ref[...])
        n_iota = lax.broadcasted_iota(jnp.int32, (8, blk), 0)
        for h in range(MOBA_HEADS):
            g = gate[h * 8:(h + 1) * 8, :]
            rank = jnp.zeros((8, blk), jnp.int32)
            for m in range(nblk):
                gm = g[m:m + 1, :]
                ahead = (gm > g) | ((gm == g) & (m < n_iota))
                rank = rank + jnp.where(ahead, 1, 0) * (m < j).astype(jnp.int32)
            pen_ref[h * 8:(h + 1) * 8, :] = jnp.where((n_iota < j) & (rank >= MOBA_TOPK), NEG_INF, 0.0)

    pen_t = pen_ref[...].T

    lane = lax.broadcasted_iota(jnp.int32, (blk, LANES), 1)
    for h in range(MOBA_HEADS):
        half = h % 2
        q_slab = q_ref[:, (h // 2) * LANES:(h // 2 + 1) * LANES]
        in_head = (lane >= half * HEAD_DIM) & (lane < (half + 1) * HEAD_DIM)
        aux = _query_aux(crow_ref[h:h + 1, :], slopes[h], j, pen_t, AUX_PEN0 + h * 8)
        qaug_ref[h] = jnp.concatenate([jnp.where(in_head, q_slab, 0).astype(BF16), aux], axis=1)

    cols = [slice((h // 2) * LANES, (h // 2 + 1) * LANES) for h in range(MOBA_HEADS)]
    outs = _attend(qaug_ref, k_ref, v_ref, kaux_ref, vaux_ref, s_ref, sown_ref, mrun_ref, acc_ref, j, cols, cols)
    o = jnp.concatenate([jnp.where(lane < HEAD_DIM, outs[2 * p], outs[2 * p + 1]) for p in range(n_slabs)], axis=1)
    o_ref[...] = _rms(o, g_ref[...]).astype(BF16)


def _moba(qa, ka, va, key_aux, val_aux, g, batch, seq):
    nblk = seq // MOBA_BLOCK
    w = MOBA_WIDTH
    crow = _query_aux_rows(_alibi_slopes(MOBA_HEADS))
    whole = lambda arr: pl.BlockSpec(arr.shape, lambda b, j: (0, 0))
    return pl.pallas_call(
        _moba_kernel,
        grid=(batch, nblk),
        in_specs=[pl.BlockSpec((MOBA_BLOCK, w), lambda b, j: (b * nblk + j, 0)),
                  pl.BlockSpec((seq, w), lambda b, j: (b, 0)),
                  pl.BlockSpec((seq, w), lambda b, j: (b, 0)),
                  whole(key_aux), whole(val_aux), whole(crow), whole(g)],
        out_specs=pl.BlockSpec((MOBA_BLOCK, w), lambda b, j: (b * nblk + j, 0)),
        out_shape=jax.ShapeDtypeStruct((batch * seq, w), BF16),
        scratch_shapes=[pltpu.VMEM((8, w), F32), pltpu.VMEM((8 * MOBA_HEADS, w), F32),
                        pltpu.VMEM((LANES, MOBA_BLOCK), F32)]
        + _attend_scratch(MOBA_HEADS, nblk),
        compiler_params=_params("arbitrary", "arbitrary"),
        name="moba",
    )(qa, ka, va, key_aux, val_aux, crow, g)


def _diff_kernel(q_ref, k_ref, v_ref, kaux_ref, vaux_ref, crow_ref, lam_ref, g_ref, o_ref,
                 qaug_ref, s_ref, sown_ref, mrun_ref, acc_ref, *, lam_init):
    j = pl.program_id(1)
    blk = MOBA_BLOCK
    slopes = _alibi_slopes(DIFF_HEADS)
    lane = lax.broadcasted_iota(jnp.int32, (blk, LANES), 1)

    lv = lam_ref[...]
    lam = (jnp.exp(jnp.sum(lv[0:1] * lv[1:2], axis=1, keepdims=True))
           - jnp.exp(jnp.sum(lv[2:3] * lv[3:4], axis=1, keepdims=True)) + lam_init)

    for i in range(2 * DIFF_HEADS):
        h, c = i // 2, i % 2
        q_slab = q_ref[:, h * LANES:(h + 1) * LANES]
        in_map = (lane >= c * HEAD_DIM) & (lane < (c + 1) * HEAD_DIM)
        aux = _query_aux(crow_ref[i:i + 1, :], slopes[h], j)
        qaug_ref[i] = jnp.concatenate([jnp.where(in_map, q_slab, 0).astype(BF16), aux], axis=1)

    cols = [slice((i // 2) * LANES, (i // 2 + 1) * LANES) for i in range(2 * DIFF_HEADS)]
    outs = _attend(qaug_ref, k_ref, v_ref, kaux_ref, vaux_ref, s_ref, sown_ref, mrun_ref, acc_ref, j, cols, cols)
    for h in range(DIFF_HEADS):
        o = outs[2 * h] - lam * outs[2 * h + 1]
        o_ref[:, h * LANES:(h + 1) * LANES] = (_rms(o, g_ref[...]) * (1.0 - lam_init)).astype(BF16)


def _diff(qd, kd, vd, key_aux, val_aux, lams, g, batch, seq, lam_init):
    nblk = seq // MOBA_BLOCK
    w = DIFF_WIDTH
    crow = _query_aux_rows([s for s in _alibi_slopes(DIFF_HEADS) for _ in range(2)])
    whole = lambda arr: pl.BlockSpec(arr.shape, lambda b, j: (0, 0))
    return pl.pallas_call(
        functools.partial(_diff_kernel, lam_init=lam_init),
        grid=(batch, nblk),
        in_specs=[pl.BlockSpec((MOBA_BLOCK, w), lambda b, j: (b * nblk + j, 0)),
                  pl.BlockSpec((seq, w), lambda b, j: (b, 0)),
                  pl.BlockSpec((seq, w), lambda b, j: (b, 0)),
                  whole(key_aux), whole(val_aux), whole(crow), whole(lams), whole(g)],
        out_specs=pl.BlockSpec((MOBA_BLOCK, w), lambda b, j: (b * nblk + j, 0)),
        out_shape=jax.ShapeDtypeStruct((batch * seq, w), BF16),
        scratch_shapes=_attend_scratch(2 * DIFF_HEADS, nblk),
        compiler_params=_params("arbitrary", "arbitrary"),
        name="diff",
    )(qd, kd, vd, key_aux, val_aux, crow, lams, g)


def _outproj_kernel(a_ref, d_ref, x_ref, wo_ref, g_ref, wq_ref, h_ref, q_ref, *, q_scale):
    wa = a_ref.shape[1]
    h = (x_ref[...]
         + jnp.dot(a_ref[...], wo_ref[:wa, :], preferred_element_type=F32)
         + jnp.dot(d_ref[...], wo_ref[wa:, :], preferred_element_type=F32))
    h_ref[...] = h
    qn = _rms(h, g_ref[...]).astype(BF16)
    q_ref[...] = (jnp.dot(qn, wq_ref[...], preferred_element_type=F32) * q_scale).astype(BF16)


def _outproj(a, d, x2, w_out, g, w_q, q_scale):
    t, dm = x2.shape
    tile = lambda w: pl.BlockSpec((TM_PROJ, w), lambda i: (i, 0))
    whole = lambda arr: pl.BlockSpec(arr.shape, lambda i: (0, 0))
    return pl.pallas_call(
        functools.partial(_outproj_kernel, q_scale=q_scale),
        grid=(t // TM_PROJ,),
        in_specs=[tile(a.shape[1]), tile(d.shape[1]), tile(dm), whole(w_out), whole(g), whole(w_q)],
        out_specs=[tile(dm), tile(dm)],
        out_shape=[jax.ShapeDtypeStruct((t, dm), F32), jax.ShapeDtypeStruct((t, dm), BF16)],
        compiler_params=_params("arbitrary"),
        name="outproj",
    )(a, d, x2, w_out, g, w_q)


def _memkv_kernel(m_ref, g_ref, w_ref, kv_ref):
    mn = _rms(m_ref[...], g_ref[...]).astype(BF16)
    kv_ref[...] = jnp.dot(mn, w_ref[...], preferred_element_type=F32).astype(BF16)


def _memkv(mem2, g, w_kv, mem_len):
    rows, dm = mem2.shape
    return pl.pallas_call(
        _memkv_kernel,
        grid=(rows // mem_len,),
        in_specs=[pl.BlockSpec((mem_len, dm), lambda i: (i, 0)),
                  pl.BlockSpec((1, dm), lambda i: (0, 0)),
                  pl.BlockSpec(w_kv.shape, lambda i: (0, 0))],
        out_specs=pl.BlockSpec((mem_len, w_kv.shape[1]), lambda i: (i, 0)),
        out_shape=jax.ShapeDtypeStruct((rows, w_kv.shape[1]), BF16),
        compiler_params=_params("arbitrary"),
        name="memkv",
    )(mem2, g, w_kv)


def _cross_kernel(q_ref, kv_ref, h_ref, wo_ref, g_ref, wr_ref, br_ref,
                  h2_ref, xn_ref, route_ref, counts_ref, run_ref):
    first = (pl.program_id(0) == 0) & (pl.program_id(1) == 0)

    @pl.when(first)
    def _():
        run_ref[...] = jnp.zeros_like(run_ref)

    tq, dm = h_ref.shape
    dh = dm // MEM_HEADS
    heads = []
    for h in range(MEM_HEADS):
        s = _dot_nt(q_ref[:, h * dh:(h + 1) * dh], kv_ref[:, h * dh:(h + 1) * dh])
        m = jnp.max(s, axis=-1, keepdims=True)
        e = jnp.exp(s - m)
        l = jnp.sum(e, axis=-1, keepdims=True)
        o = jnp.dot(e.astype(BF16), kv_ref[:, dm + h * dh:dm + (h + 1) * dh], preferred_element_type=F32)
        heads.append((o * (1.0 / l)).astype(BF16))
    o = jnp.concatenate(heads, axis=1)
    h2 = h_ref[...] + jnp.dot(o, wo_ref[...], preferred_element_type=F32)
    h2_ref[...] = h2
    xn = _rms(h2, g_ref[...])
    _to_token_major(xn_ref, xn)

    x_hi = xn.astype(BF16)
    x_lo = (xn - x_hi.astype(F32)).astype(BF16)
    hi = jnp.dot(x_hi, wr_ref[...], preferred_element_type=F32)
    lo_hi = jnp.dot(x_lo, wr_ref[:, :ROUTE_COLS], preferred_element_type=F32)
    logits = hi[:, :ROUTE_COLS] + (hi[:, ROUTE_COLS:] + lo_hi) + br_ref[...]
    lane = lax.broadcasted_iota(jnp.int32, logits.shape, 1).astype(F32)
    big = float(ROUTE_COLS)

    def top1(vals):
        v = jnp.max(vals, axis=-1, keepdims=True)
        i = jnp.min(jnp.where(vals == v, lane, big), axis=-1, keepdims=True)
        return v, i

    gl = jnp.where(lane < N_GROUPS, logits, -jnp.inf)
    g_max, g_idx = top1(gl)
    g_w = 1.0 / jnp.sum(jnp.exp(gl - g_max), axis=-1, keepdims=True)
    lo = EXPERT_COL0 + g_idx * EXPERTS_PER_GROUP
    el = jnp.where((lane >= lo) & (lane < lo + EXPERTS_PER_GROUP), logits, -jnp.inf)
    v1, i1 = top1(el)
    v2, i2 = top1(jnp.where(lane == i1, -jnp.inf, el))
    e2 = jnp.exp(v2 - v1)
    w1 = g_w * (1.0 / (1.0 + e2))
    w2 = g_w * (e2 / (1.0 + e2))

    onehot = jnp.where((lane == i1) | (lane == i2), 1.0, 0.0)
    run_ref[...] = run_ref[...] + jnp.sum(onehot, axis=0, keepdims=True)
    counts_ref[...] = run_ref[...]

    rec = jnp.zeros(logits.shape, F32)
    for col, val in enumerate((i1 - EXPERT_COL0, i2 - EXPERT_COL0, w1, w2)):
        rec = jnp.where(lane == col, val, rec)
    route_ref[...] = rec


def _cross(qc, kv, h1, w_o, g, w_r, b_r, batch, seq, mem_len):
    t, dm = h1.shape
    nt = seq // TQ_CROSS
    tile = lambda w: pl.BlockSpec((TQ_CROSS, w), lambda b, i: (b * nt + i, 0))
    whole = lambda arr: pl.BlockSpec(arr.shape, lambda b, i: (0, 0))
    return pl.pallas_call(
        _cross_kernel,
        grid=(batch, nt),
        in_specs=[tile(dm), pl.BlockSpec((mem_len, kv.shape[1]), lambda b, i: (b, 0)), tile(dm),
                  whole(w_o), whole(g), whole(w_r), whole(b_r)],
        out_specs=[tile(dm), pl.BlockSpec((TQ_CROSS * SUB, LANES), lambda b, i: (b * nt + i, 0)),
                   tile(ROUTE_COLS), pl.BlockSpec((1, ROUTE_COLS), lambda b, i: (0, 0))],
        out_shape=[jax.ShapeDtypeStruct((t, dm), F32), jax.ShapeDtypeStruct((t * SUB, LANES), F32),
                   jax.ShapeDtypeStruct((t, ROUTE_COLS), F32), jax.ShapeDtypeStruct((1, ROUTE_COLS), F32)],
        scratch_shapes=[pltpu.VMEM((1, ROUTE_COLS), F32)],
        compiler_params=_params("arbitrary", "arbitrary"),
        name="cross",
    )(qc, kv, h1, w_o, g, w_r, b_r)


SUB = 8


def _to_token_major(ref, x):
    rows = x.shape[0]
    for c in range(x.shape[1] // LANES):
        ref[pl.ds(c, rows, stride=SUB), :] = x[:, c * LANES:(c + 1) * LANES]


def _from_token_major(ref, rows):
    n_chunks = ref.shape[0] // rows
    return jnp.concatenate([ref[pl.ds(c, rows, stride=SUB), :] for c in range(n_chunks)], axis=1)


def _experts_kernel(te_ref, nused_ref, cs_ref, nv_ref, rows_ref, xn_ref, wg_ref, wu_ref, wd_ref, yo_ref,
                    xbuf0, xbuf1, ybuf0, ybuf1, wgu_bf, wd_bf, gsem, ssem, *, n_tok):
    i = pl.program_id(0)
    last = pl.num_programs(0) - 1
    n_used = nused_ref[0]
    xbuf, ybuf = (xbuf0, xbuf1), (ybuf0, ybuf1)
    tm = xbuf0.shape[0] // SUB
    ff = wd_ref.shape[0]
    token_rows = lambda tok: pl.ds(pl.multiple_of(tok * SUB, SUB), SUB)

    def gather(tile, buf):
        base = cs_ref[tile]
        for s in range(tm):
            row = rows_ref[jnp.minimum(base + s, 2 * n_tok - 1)]
            pltpu.make_async_copy(xn_ref.at[token_rows(row & (n_tok - 1))], xbuf[buf].at[token_rows(s)],
                                  gsem.at[buf]).start(priority=s % 2)

    def scatter(tile, buf):
        base = cs_ref[tile]

        def copy(s):
            return pltpu.make_async_copy(ybuf[buf].at[token_rows(s)], yo_ref.at[token_rows(rows_ref[base + s])],
                                         ssem.at[buf])

        @pl.when(nv_ref[tile] == tm)
        def _():
            for s in range(tm):
                copy(s).start(priority=s % 2)

        @pl.when(nv_ref[tile] < tm)
        def _():
            def body(s, c):
                copy(s).start()
                return c
            lax.fori_loop(0, nv_ref[tile], body, 0)

    @pl.when((i < n_used) & ((i == 0) | (te_ref[i] != te_ref[jnp.maximum(i - 1, 0)])))
    def _():
        wgu_bf[:, :ff] = wg_ref[...].astype(BF16)
        wgu_bf[:, ff:] = wu_ref[...].astype(BF16)
        wd_bf[...] = wd_ref[...].astype(BF16)

    def wait_gather(buf):
        pltpu.make_async_copy(xn_ref.at[pl.ds(0, tm * SUB)], xbuf[buf], gsem.at[buf]).wait()

    def wait_scatter(tile, buf):
        @pl.when(nv_ref[tile] == tm)
        def _():
            pltpu.make_async_copy(ybuf[buf], yo_ref.at[pl.ds(0, tm * SUB)], ssem.at[buf]).wait()

        @pl.when(nv_ref[tile] < tm)
        def _():
            def body(s, c):
                pltpu.make_async_copy(ybuf[buf].at[pl.ds(0, SUB)], yo_ref.at[pl.ds(0, SUB)], ssem.at[buf]).wait()
                return c
            lax.fori_loop(0, nv_ref[tile], body, 0)

    def step(cur):
        nxt = 1 - cur

        @pl.when((i == 0) & (n_used > 0))
        def _():
            gather(0, cur)

        @pl.when((i >= 2) & (i - 2 < n_used))
        def _():
            wait_scatter(i - 2, cur)

        @pl.when(i < n_used)
        def _():
            wait_gather(cur)
            gather(jnp.minimum(i + 1, n_used - 1), nxt)
            x = _from_token_major(xbuf[cur], tm).astype(BF16)
            gu = jnp.dot(x, wgu_bf[...], preferred_element_type=F32)
            g = gu[:, :ff]
            hh = (g * jax.nn.sigmoid(g)) * gu[:, ff:]
            _to_token_major(ybuf[cur], jnp.dot(hh.astype(BF16), wd_bf[...], preferred_element_type=F32))
            scatter(i, cur)

            @pl.when(i + 1 == n_used)
            def _():
                wait_gather(nxt)

        @pl.when(i == last)
        def _():
            @pl.when((i >= 1) & (i - 1 < n_used))
            def _():
                wait_scatter(i - 1, nxt)

            @pl.when(i < n_used)
            def _():
                wait_scatter(i, cur)

    for parity in range(2):
        pl.when(i % 2 == parity)(functools.partial(step, parity))


def _experts(tile_expert, n_used, tile_start, tile_valid, sorted_rows, xn_tm, w_gate, w_up, w_down, tm):
    n_tok = xn_tm.shape[0] // SUB
    n_tiles = tile_expert.shape[0]
    _, dm, ff = w_gate.shape
    assert n_tok & (n_tok - 1) == 0, "source token = output row & (n_tok - 1)"
    assert dm == SUB * LANES
    w_map = lambda i, te, nu, cs, nv, sr: (te[i], 0, 0)
    return pl.pallas_call(
        functools.partial(_experts_kernel, n_tok=n_tok),
        grid_spec=pltpu.PrefetchScalarGridSpec(
            num_scalar_prefetch=5,
            grid=(n_tiles,),
            in_specs=[pl.BlockSpec(memory_space=pl.ANY),
                      pl.BlockSpec((None, dm, ff), w_map), pl.BlockSpec((None, dm, ff), w_map),
                      pl.BlockSpec((None, ff, dm), w_map)],
            out_specs=pl.BlockSpec(memory_space=pl.ANY),
            scratch_shapes=[pltpu.VMEM((tm * SUB, LANES), F32)] * 4
            + [pltpu.VMEM((dm, 2 * ff), BF16), pltpu.VMEM((ff, dm), BF16),
                            pltpu.SemaphoreType.DMA((2,)), pltpu.SemaphoreType.DMA((2,))]),
        out_shape=jax.ShapeDtypeStruct((2 * n_tok * SUB, LANES), F32),
        compiler_params=_params("arbitrary"),
        name="experts",
    )(tile_expert, n_used, tile_start, tile_valid, sorted_rows, xn_tm, w_gate, w_up, w_down)


def _combine_kernel(h2_ref, route_ref, y0_ref, y1_ref, g_ref, o_ref):
    rows = h2_ref.shape[0]
    rec = route_ref[...]
    h = (h2_ref[...] + rec[:, 2:3] * _from_token_major(y0_ref, rows)
         + rec[:, 3:4] * _from_token_major(y1_ref, rows))
    o_ref[...] = _rms(h, g_ref[...])


def _combine(h2, route, yo, g):
    t, dm = h2.shape
    nt = t // TM_PROJ
    tile = lambda w: pl.BlockSpec((TM_PROJ, w), lambda i: (i, 0))
    return pl.pallas_call(
        _combine_kernel,
        grid=(nt,),
        in_specs=[tile(dm), tile(ROUTE_COLS),
                  pl.BlockSpec((TM_PROJ * SUB, LANES), lambda i: (i, 0)),
                  pl.BlockSpec((TM_PROJ * SUB, LANES), lambda i: (i + nt, 0)),
                  pl.BlockSpec((1, dm), lambda i: (0, 0))],
        out_specs=tile(dm),
        out_shape=jax.ShapeDtypeStruct((t, dm), F32),
        compiler_params=_params("arbitrary"),
        name="combine",
    )(h2, route, yo, yo, g)


def kernel(x, mem, norm_mix, w_in, lambda_q1, lambda_k1, lambda_q2, lambda_k2, diff_subln, norm_moba_out, w_out, norm_mem_q, norm_mem_kv, w_mem_q, w_mem_kv, w_mem_o, norm_ffn, w_router_group, b_router_group, w_router_expert, b_router_expert, w_expert_gate, w_expert_up, w_expert_down, norm_final):
    batch, seq, dm = x.shape
    mem_len = mem.shape[1]
    depth = w_in.shape[0]
    t = batch * seq
    assert seq % MOBA_BLOCK == 0 and seq % TQ_CROSS == 0 and t % TM_PROJ == 0

    h = x.reshape(t, dm)
    mem2 = mem.reshape(batch * mem_len, dm)
    row = lambda v: v.reshape(1, -1).astype(F32)
    for l in range(depth):
        lam_init = 0.8 - 0.6 * math.exp(-0.3 * l)
        q_scale = HEAD_DIM ** -0.5 * LOG2E
        qa, ka, va, qd, kd, vd = _inproj(h, row(norm_mix[l]), w_in[l].astype(BF16),
                                         (q_scale, 1.0, 1.0, q_scale, 1.0, 1.0))
        key_aux, val_aux = _key_aux(seq), _value_aux()
        a = _moba(qa, ka, va, key_aux, val_aux, row(norm_moba_out[l]), batch, seq)
        lams = jnp.stack([lambda_q1[l], lambda_k1[l], lambda_q2[l], lambda_k2[l]]).astype(F32)
        d = _diff(qd, kd, vd, key_aux, val_aux, lams, row(diff_subln[l]), batch, seq, lam_init)
        h1, qc = _outproj(a, d, h, w_out[l].astype(BF16), row(norm_mem_q[l]), w_mem_q[l].astype(BF16),
                          (dm // MEM_HEADS) ** -0.5)
        kv = _memkv(mem2, row(norm_mem_kv[l]), w_mem_kv[l].astype(BF16), mem_len)

        pad = ROUTE_COLS - N_GROUPS - N_EXPERTS
        w_r = jnp.pad(jnp.concatenate([w_router_group[l], w_router_expert[l]], axis=1).astype(F32), ((0, 0), (0, pad)))
        w_r_hi = w_r.astype(BF16)
        w_r = jnp.concatenate([w_r_hi, (w_r - w_r_hi.astype(F32)).astype(BF16)], axis=1)
        b_r = jnp.pad(jnp.concatenate([b_router_group[l], b_router_expert[l]]).astype(F32), (0, pad)).reshape(1, -1)
        h2, xn, route, counts = _cross(qc, kv, h1, w_mem_o[l].astype(BF16), row(norm_ffn[l]), w_r, b_r,
                                       batch, seq, mem_len)

        tm = TM_EXPERT
        n_tiles = (2 * t) // tm + N_EXPERTS
        cnt = counts[0, EXPERT_COL0:EXPERT_COL0 + N_EXPERTS].astype(jnp.int32)
        tiles_of = (cnt + tm - 1) // tm
        tile_end = jnp.cumsum(tiles_of)
        tile_idx = jnp.arange(n_tiles, dtype=jnp.int32)
        tile_expert = jnp.minimum(jnp.sum((tile_end[None, :] <= tile_idx[:, None]).astype(jnp.int32), axis=1),
                                  N_EXPERTS - 1)
        n_used = tile_end[-1:].astype(jnp.int32)
        onehot_te = (tile_expert[:, None] == jnp.arange(N_EXPERTS, dtype=jnp.int32)[None, :]).astype(jnp.int32)
        pick = lambda table: jnp.sum(onehot_te * table[None, :], axis=1)
        in_expert = (tile_idx - pick(tile_end - tiles_of)) * tm
        tile_start = pick(jnp.cumsum(cnt) - cnt) + in_expert
        tile_valid = jnp.where(tile_idx < n_used[0], jnp.clip(pick(cnt) - in_expert, 0, tm), 0).astype(jnp.int32)
        out_row = jnp.arange(t, dtype=jnp.int32)[:, None] + jnp.array([0, t], jnp.int32)[None, :]
        keys = route[:, 0:2].astype(jnp.int32) * (2 * t) + out_row
        sorted_rows = jnp.sort(keys.reshape(-1)) & (2 * t - 1)

        yo = _experts(tile_expert, n_used, tile_start.astype(jnp.int32), tile_valid, sorted_rows, xn,
                      w_expert_gate[l].astype(F32), w_expert_up[l].astype(F32), w_expert_down[l].astype(F32), tm)
        assert depth == 1
        h = _combine(h2, route, yo, row(norm_final))
    return h.reshape(batch, seq, dm)
```
